```python
import jax, jax.numpy as jnp
from jax import lax
import numpy as np

D_MODEL = 1024
BATCH = 8
SEQ = 4096
DEPTH = 2

MLA_HEADS = 8
MLA_NOPE = 64
MLA_ROPE = 32
MLA_V = 64
MLA_Q_RANK = 256
MLA_KV_RANK = 128
NSA_HEADS = 8
NSA_KV_HEADS = 2
NSA_HPG = NSA_HEADS // NSA_KV_HEADS
NSA_HD = 64
NSA_ROT = NSA_HD // 4
L_CMP = 32
S_CMP = 16
CMP_HIDDEN = 128
L_SLC = 64
N_SEL = 16
WINDOW = 512
ROPE_THETA = 500000.0
Q_BLOCK = 128
D_FF = 2816
ALPHA = (2 * DEPTH) ** 0.25
BETA = (8 * DEPTH) ** -0.25
LN_EPS = 1e-5
RMS_EPS = 1e-6
NEG = -1e30
BIG = 1e30
SPLITS = (MLA_Q_RANK, MLA_KV_RANK, MLA_ROPE, NSA_HEADS * NSA_HD, 6 * NSA_KV_HEADS * NSA_HD, 3 * NSA_HEADS, D_MODEL, D_MODEL)
D_IN = sum(SPLITS)

kernel_name = 'hybrid_mla_nsa_macaron_deepnorm'


def layer_norm(x, g, b):
    xf = x.astype(jnp.float32)
    mu = jnp.mean(xf, -1, keepdims=True)
    var = jnp.mean(jnp.square(xf - mu), -1, keepdims=True)
    return ((xf - mu) * lax.rsqrt(var + LN_EPS) * g + b).astype(x.dtype)


def rms_norm(x, g):
    xf = x.astype(jnp.float32)
    return (xf * lax.rsqrt(jnp.mean(xf * xf, -1, keepdims=True) + RMS_EPS) * g).astype(x.dtype)


def swiglu(x, wg, wu, wd):
    return (jax.nn.silu(x @ wg) * (x @ wu)) @ wd


def rope_tables(rot_dim, T):
    inv = 1.0 / (ROPE_THETA ** (jnp.arange(0, rot_dim, 2, dtype=jnp.float32) / rot_dim))
    ang = jnp.arange(T, dtype=jnp.float32)[:, None] * inv[None, :]
    return jnp.cos(ang), jnp.sin(ang)


def apply_rope(x, cos, sin):
    x1, x2 = jnp.split(x, 2, axis=-1)
    c, s = cos[:, None, :], sin[:, None, :]
    return jnp.concatenate([x1 * c - x2 * s, x2 * c + x1 * s], -1).astype(x.dtype)


def partial_rope(x, cos, sin):
    return jnp.concatenate([apply_rope(x[..., :NSA_ROT], cos, sin), x[..., NSA_ROT:]], -1)


def to_blocks(a):
    B, T = a.shape[:2]
    return a.reshape((B, T // Q_BLOCK, Q_BLOCK) + a.shape[2:]).swapaxes(0, 1)


def from_blocks(a):
    nblk, B, Q = a.shape[:3]
    return a.swapaxes(0, 1).reshape((B, nblk * Q) + a.shape[3:])


def mla_attention(q_nope, q_pe, k_nope, k_pe, v):
    T = q_nope.shape[1]
    scale = (MLA_NOPE + MLA_ROPE) ** -0.5
    kpos = jnp.arange(T)

    def one_block(args):
        qs, qn, qp = args
        s = (jnp.einsum('bqhd,bkhd->bhqk', qn, k_nope, preferred_element_type=jnp.float32)
             + jnp.einsum('bqhd,bkd->bhqk', qp, k_pe, preferred_element_type=jnp.float32)) * scale
        qpos = qs + jnp.arange(Q_BLOCK)
        s = jnp.where(kpos[None, :] <= qpos[:, None], s, NEG)
        p = jax.nn.softmax(s, axis=-1).astype(v.dtype)
        return jnp.einsum('bhqk,bkhd->bqhd', p, v)

    starts = jnp.arange(T // Q_BLOCK) * Q_BLOCK
    return from_blocks(lax.map(one_block, (starts, to_blocks(q_nope), to_blocks(q_pe))))


def compress(kv, pe, w1, b1, w2):
    B, T, G, D = kv.shape
    n_cmp = (T - L_CMP) // S_CMP + 1
    idx = np.arange(n_cmp)[:, None] * S_CMP + np.arange(L_CMP)[None, :]
    blk = kv[:, idx] + pe[:, None, :]
    blk = blk.transpose(0, 1, 3, 2, 4).reshape(B, n_cmp, G, L_CMP * D)
    return jax.nn.gelu(blk @ w1 + b1) @ w2


def nsa_attention(q, k_c, v_c, k_s, v_s, k_w, v_w, gates,
                  pe_k, k_w1, k_b1, k_w2, pe_v, v_w1, v_b1, v_w2):
    B, T = q.shape[:2]
    G, HPG, D = NSA_KV_HEADS, NSA_HPG, NSA_HD
    scale = D ** -0.5
    qg = q.reshape(B, T, G, HPG, D)
    tpos = jnp.arange(T)
    n_cmp = (T - L_CMP) // S_CMP + 1
    n_slc = T // L_SLC
    n_sel = min(N_SEL, n_slc)

    kc = compress(k_c, pe_k, k_w1, k_b1, k_w2)
    vc = compress(v_c, pe_v, v_w1, v_b1, v_w2)
    cmp_start = np.arange(n_cmp) * S_CMP
    valid = jnp.asarray(cmp_start + L_CMP - 1)[None, :] <= tpos[:, None]
    s = jnp.einsum('btghd,bcgd->bghtc', qg, kc, preferred_element_type=jnp.float32) * scale
    p_cmp = jax.nn.softmax(jnp.where(valid, s, NEG), axis=-1) * valid
    o_c = jnp.einsum('bghtc,bcgd->btghd', p_cmp.astype(vc.dtype), vc)

    blk = np.arange(n_slc)
    overlap = ((cmp_start[:, None] < (blk[None, :] + 1) * L_SLC)
               & (cmp_start[:, None] + L_CMP > blk[None, :] * L_SLC)).astype(np.float32)
    imp = jnp.einsum('bghtc,cj->btgj', p_cmp, jnp.asarray(overlap))
    cur = (tpos // L_SLC)[:, None]
    jb = jnp.arange(n_slc)[None, :]
    forced = (jb == 0) | (jb == cur) | (jb == cur - 1)
    causal = jb <= cur
    imp = jnp.where(forced[:, None], BIG, jnp.where(causal[:, None], imp, NEG))
    _, sel = lax.top_k(imp, n_sel)

    k_sb = k_s.reshape(B, n_slc, L_SLC, G, D).transpose(0, 3, 1, 2, 4)
    v_sb = v_s.reshape(B, n_slc, L_SLC, G, D).transpose(0, 3, 1, 2, 4)
    k_wp = jnp.pad(k_w, ((0, 0), (WINDOW, 0), (0, 0), (0, 0)))
    v_wp = jnp.pad(v_w, ((0, 0), (WINDOW, 0), (0, 0), (0, 0)))
    span = WINDOW + Q_BLOCK
    bi = jnp.arange(B)[:, None, None, None]
    gi = jnp.arange(G)[None, None, :, None]
    n_keys = n_sel * L_SLC

    def one_block(args):
        qs, qb, sb = args
        qpos = qs + jnp.arange(Q_BLOCK)
        ks = k_sb[bi, gi, sb].reshape(B, Q_BLOCK, G, n_keys, D)
        vs = v_sb[bi, gi, sb].reshape(B, Q_BLOCK, G, n_keys, D)
        kpos = (sb[..., None] * L_SLC + jnp.arange(L_SLC)).reshape(B, Q_BLOCK, G, n_keys)
        m_s = (kpos <= qpos[None, :, None, None]).transpose(0, 2, 1, 3)[:, :, None]
        s_s = jnp.einsum('bqghd,bqgkd->bghqk', qb, ks, preferred_element_type=jnp.float32) * scale
        p_s = jax.nn.softmax(jnp.where(m_s, s_s, NEG), axis=-1).astype(vs.dtype)
        o_s = jnp.einsum('bghqk,bqgkd->bqghd', p_s, vs)
        kw = lax.dynamic_slice_in_dim(k_wp, qs, span, axis=1)
        vw = lax.dynamic_slice_in_dim(v_wp, qs, span, axis=1)
        kpos_w = qs - WINDOW + jnp.arange(span)
        m_w = ((kpos_w[None, :] <= qpos[:, None]) & (kpos_w[None, :] > qpos[:, None] - WINDOW)
               & (kpos_w[None, :] >= 0))
        s_w = jnp.einsum('bqghd,bkgd->bghqk', qb, kw, preferred_element_type=jnp.float32) * scale
        p_w = jax.nn.softmax(jnp.where(m_w, s_w, NEG), axis=-1).astype(vw.dtype)
        o_w = jnp.einsum('bghqk,bkgd->bqghd', p_w, vw)
        return o_s, o_w

    starts = jnp.arange(T // Q_BLOCK) * Q_BLOCK
    o_s, o_w = lax.map(one_block, (starts, to_blocks(qg), to_blocks(sel)))
    o_s, o_w = from_blocks(o_s), from_blocks(o_w)
    g = gates.reshape(B, T, G, HPG, 3)
    o = g[..., 0:1] * o_c + g[..., 1:2] * o_s + g[..., 2:3] * o_w
    return o.reshape(B, T, NSA_HEADS * D)


def token_mixer(x, w_in, q_norm_g, w_uq, kv_norm_g, w_ukv,
                cmp_pe_k, cmp_k_w1, cmp_k_b1, cmp_k_w2, cmp_pe_v, cmp_v_w1, cmp_v_b1, cmp_v_w2,
                w_proj_mla, w_proj_nsa, w_out, cos_m, sin_m, cos_n, sin_n):
    B, T, _ = x.shape
    offsets = np.cumsum(SPLITS)[:-1].tolist()
    c_q, c_kv, k_pe, q_n, kv_n, g_n, gate_m, gate_n = jnp.split(x @ w_in, offsets, axis=-1)

    q = (rms_norm(c_q, q_norm_g) @ w_uq).reshape(B, T, MLA_HEADS, MLA_NOPE + MLA_ROPE)
    q_nope, q_pe = q[..., :MLA_NOPE], apply_rope(q[..., MLA_NOPE:], cos_m, sin_m)
    kv = (rms_norm(c_kv, kv_norm_g) @ w_ukv).reshape(B, T, MLA_HEADS, MLA_NOPE + MLA_V)
    k_nope, v_m = kv[..., :MLA_NOPE], kv[..., MLA_NOPE:]
    k_pe = apply_rope(k_pe[:, :, None, :], cos_m, sin_m)[:, :, 0]
    o_m = mla_attention(q_nope, q_pe, k_nope, k_pe, v_m).reshape(B, T, MLA_HEADS * MLA_V)

    qn = partial_rope(q_n.reshape(B, T, NSA_HEADS, NSA_HD), cos_n, sin_n)
    kvs = kv_n.reshape(B, T, 6, NSA_KV_HEADS, NSA_HD)
    k_c = partial_rope(kvs[:, :, 0], cos_n, sin_n)
    k_s = partial_rope(kvs[:, :, 2], cos_n, sin_n)
    k_w = partial_rope(kvs[:, :, 4], cos_n, sin_n)
    gates = jax.nn.sigmoid(g_n.reshape(B, T, NSA_HEADS, 3))
    o_n = nsa_attention(qn, k_c, kvs[:, :, 1], k_s, kvs[:, :, 3], k_w, kvs[:, :, 5], gates,
                        cmp_pe_k, cmp_k_w1, cmp_k_b1, cmp_k_w2, cmp_pe_v, cmp_v_w1, cmp_v_b1, cmp_v_w2)

    y = jax.nn.sigmoid(gate_m) * (o_m @ w_proj_mla) + jax.nn.sigmoid(gate_n) * (o_n @ w_proj_nsa)
    return y @ w_out


def setup_inputs(seed: int = 0) -> dict:
    key = jax.random.key(seed)
    ks = iter(jax.random.split(key, 64))

    def w(shape, fan_in, scale=1.0):
        return jax.random.normal(next(ks), (DEPTH,) + shape, jnp.float32) * (scale * fan_in ** -0.5)

    def gain(n):
        return 1.0 + 0.02 * jax.random.normal(next(ks), (DEPTH, n), jnp.float32)

    def small(shape, s=0.02):
        return s * jax.random.normal(next(ks), (DEPTH,) + shape, jnp.float32)

    return {
        'x': jax.random.normal(next(ks), (BATCH, SEQ, D_MODEL), jnp.float32),
        'ln_f1_g': gain(D_MODEL), 'ln_f1_b': small((D_MODEL,)),
        'ffn1_wg': w((D_MODEL, D_FF), D_MODEL), 'ffn1_wu': w((D_MODEL, D_FF), D_MODEL),
        'ffn1_wd': w((D_FF, D_MODEL), D_FF, BETA),
        'w_in': w((D_MODEL, D_IN), D_MODEL),
        'q_norm_g': gain(MLA_Q_RANK), 'w_uq': w((MLA_Q_RANK, MLA_HEADS * (MLA_NOPE + MLA_ROPE)), MLA_Q_RANK),
        'kv_norm_g': gain(MLA_KV_RANK), 'w_ukv': w((MLA_KV_RANK, MLA_HEADS * (MLA_NOPE + MLA_V)), MLA_KV_RANK),
        'cmp_pe_k': small((L_CMP, NSA_HD), 0.1), 'cmp_k_w1': w((L_CMP * NSA_HD, CMP_HIDDEN), L_CMP * NSA_HD),
        'cmp_k_b1': small((CMP_HIDDEN,)), 'cmp_k_w2': w((CMP_HIDDEN, NSA_HD), CMP_HIDDEN),
        'cmp_pe_v': small((L_CMP, NSA_HD), 0.1), 'cmp_v_w1': w((L_CMP * NSA_HD, CMP_HIDDEN), L_CMP * NSA_HD),
        'cmp_v_b1': small((CMP_HIDDEN,)), 'cmp_v_w2': w((CMP_HIDDEN, NSA_HD), CMP_HIDDEN),
        'w_proj_mla': w((MLA_HEADS * MLA_V, D_MODEL), MLA_HEADS * MLA_V),
        'w_proj_nsa': w((NSA_HEADS * NSA_HD, D_MODEL), NSA_HEADS * NSA_HD),
        'w_out': w((D_MODEL, D_MODEL), D_MODEL, BETA),
        'ln_mix_g': gain(D_MODEL), 'ln_mix_b': small((D_MODEL,)),
        'ffn2_wg': w((D_MODEL, D_FF), D_MODEL), 'ffn2_wu': w((D_MODEL, D_FF), D_MODEL),
        'ffn2_wd': w((D_FF, D_MODEL), D_FF, BETA),
        'ln_f2_g': gain(D_MODEL), 'ln_f2_b': small((D_MODEL,)),
    }


def reference(x, ln_f1_g, ln_f1_b, ffn1_wg, ffn1_wu, ffn1_wd, w_in, q_norm_g, w_uq, kv_norm_g, w_ukv,
              cmp_pe_k, cmp_k_w1, cmp_k_b1, cmp_k_w2, cmp_pe_v, cmp_v_w1, cmp_v_b1, cmp_v_w2,
              w_proj_mla, w_proj_nsa, w_out, ln_mix_g, ln_mix_b,
              ffn2_wg, ffn2_wu, ffn2_wd, ln_f2_g, ln_f2_b):
    T = x.shape[1]
    cos_m, sin_m = rope_tables(MLA_ROPE, T)
    cos_n, sin_n = rope_tables(NSA_ROT, T)
    for l in range(DEPTH):
        x = layer_norm(ALPHA * x + 0.5 * swiglu(x, ffn1_wg[l], ffn1_wu[l], ffn1_wd[l]), ln_f1_g[l], ln_f1_b[l])
        mix = token_mixer(x, w_in[l], q_norm_g[l], w_uq[l], kv_norm_g[l], w_ukv[l],
                          cmp_pe_k[l], cmp_k_w1[l], cmp_k_b1[l], cmp_k_w2[l],
                          cmp_pe_v[l], cmp_v_w1[l], cmp_v_b1[l], cmp_v_w2[l],
                          w_proj_mla[l], w_proj_nsa[l], w_out[l], cos_m, sin_m, cos_n, sin_n)
        x = layer_norm(ALPHA * x + mix, ln_mix_g[l], ln_mix_b[l])
        x = layer_norm(ALPHA * x + 0.5 * swiglu(x, ffn2_wg[l], ffn2_wu[l], ffn2_wd[l]), ln_f2_g[l], ln_f2_b[l])
    return x
```

```python
import functools

import numpy as np
import jax
import jax.numpy as jnp
from jax import lax
from jax.experimental import pallas as pl
from jax.experimental.pallas import tpu as pltpu

D_MODEL = 1024
DEPTH = 2
MLA_HEADS = 8
MLA_NOPE = 64
MLA_ROPE = 32
MLA_V = 64
MLA_Q_RANK = 256
MLA_KV_RANK = 128
NSA_HEADS = 8
NSA_KV_HEADS = 2
NSA_HPG = NSA_HEADS // NSA_KV_HEADS
NSA_HD = 64
NSA_ROT = NSA_HD // 4
L_CMP = 32
S_CMP = 16
CMP_HIDDEN = 128
L_SLC = 64
LOG2_L_SLC = 6
N_SEL = 16
WINDOW = 512
ROPE_THETA = 500000.0
D_FF = 2816
ALPHA = (2 * DEPTH) ** 0.25
LN_EPS = 1e-5
RMS_EPS = 1e-6
NEG = -1e30
BIG = 1e30
MLA_SCALE = (MLA_NOPE + MLA_ROPE) ** -0.5
NSA_SCALE = NSA_HD ** -0.5

OFF_CQ = 0
OFF_CKV = OFF_CQ + MLA_Q_RANK
OFF_KPE = OFF_CKV + MLA_KV_RANK
OFF_NQ = OFF_KPE + MLA_ROPE
OFF_NKV = OFF_NQ + NSA_HEADS * NSA_HD
OFF_GN = OFF_NKV + 6 * NSA_KV_HEADS * NSA_HD
OFF_GM = OFF_GN + 3 * NSA_HEADS
OFF_GNS = OFF_GM + D_MODEL

LANE = 128
ONE_LANE = 64
VMEM_LIMIT = 56 * 1024 * 1024

BF = jnp.bfloat16
F32 = jnp.float32

FF_CHUNK = 1408
TM_ROWS = 512
TQ_MLA = 512
TQ_CMP = 256
TQ_NSA = 256
TK_NSA = 512


def _dot(a, b):
    return jnp.dot(a, b, preferred_element_type=F32)


def _dot_nt(a, b):
    return lax.dot_general(a, b, (((1,), (1,)), ((), ())), preferred_element_type=F32)


def _ln(y, g, b):
    mu = jnp.mean(y, -1, keepdims=True)
    d = y - mu
    var = jnp.mean(d * d, -1, keepdims=True)
    return d * lax.rsqrt(var + LN_EPS) * g + b


def _params(*sem):
    return pltpu.CompilerParams(dimension_semantics=sem, vmem_limit_bytes=VMEM_LIMIT)


def _const_spec(shape):
    n = len(shape)
    return pl.BlockSpec(shape, lambda *_: (0,) * n)


def _ffn_ln(x, wg_ref, wu_ref, wd_ref, g, b):
    xb = x.astype(BF)
    acc = jnp.zeros(x.shape, F32)
    for c in range(0, D_FF, FF_CHUNK):
        hg = _dot(xb, wg_ref[:, c:c + FF_CHUNK])
        hu = _dot(xb, wu_ref[:, c:c + FF_CHUNK])
        h = (hg * jax.nn.sigmoid(hg)) * hu
        acc = acc + _dot(h.astype(BF), wd_ref[c:c + FF_CHUNK, :])
    return _ln(ALPHA * x + 0.5 * acc, g, b)


def _ffn_kernel(x_ref, wg_ref, wu_ref, wd_ref, g_ref, b_ref, o_ref):
    o_ref[...] = _ffn_ln(x_ref[...], wg_ref, wu_ref, wd_ref, g_ref[...], b_ref[...])


def _ffn_call(x2d, wg, wu, wd, g, b):
    n, d = x2d.shape
    tm = min(TM_ROWS, n)
    return pl.pallas_call(
        _ffn_kernel,
        grid=(n // tm,),
        in_specs=[pl.BlockSpec((tm, d), lambda i: (i, 0)),
                  _const_spec(wg.shape), _const_spec(wu.shape), _const_spec(wd.shape),
                  _const_spec(g.shape), _const_spec(b.shape)],
        out_specs=pl.BlockSpec((tm, d), lambda i: (i, 0)),
        out_shape=jax.ShapeDtypeStruct((n, d), F32),
        compiler_params=_params("parallel"),
        name="ffn_ln",
    )(x2d, wg, wu, wd, g, b)


def _rope(v, c, s1, s2, half):
    return v * c + pltpu.roll(v, LANE - half, 1) * s1 + pltpu.roll(v, half, 1) * s2


def _rms(v, g):
    return v * lax.rsqrt(jnp.mean(v * v, -1, keepdims=True) + RMS_EPS) * g


def _inproj_kernel(x_ref, wa_ref, wn_ref, wq_ref, wk_ref, wv_ref, gq_ref, gkv_ref,
                   cm_ref, s1m_ref, s2m_ref, cn_ref, s1n_ref, s2n_ref,
                   qm_ref, km_ref, vm_ref, nq_ref, kc_ref, vc_ref,
                   ks_ref, vs_ref, kw_ref, vw_ref, gn_ref):
    tm = x_ref.shape[1]
    t0 = pl.program_id(1) * tm
    xb = x_ref[0].astype(BF)
    lane = lax.broadcasted_iota(jnp.int32, (tm, LANE), 1)
    row = lax.broadcasted_iota(jnp.int32, (tm, LANE), 0) + t0
    one_col = (lane == ONE_LANE).astype(F32)
    blk_onehot = (lane - ONE_LANE == (row >> LOG2_L_SLC)).astype(F32)
    cm, s1m, s2m = cm_ref[...], s1m_ref[...], s2m_ref[...]
    cn, s1n, s2n = cn_ref[...], s1n_ref[...], s2n_ref[...]
    hm = MLA_ROPE // 2
    hn = NSA_ROT // 2

    pa = _dot(xb, wa_ref[...])
    cqn = _rms(pa[:, :MLA_Q_RANK], gq_ref[...]).astype(BF)
    ckvn = _rms(pa[:, MLA_Q_RANK:MLA_Q_RANK + MLA_KV_RANK], gkv_ref[...]).astype(BF)
    kpe = _rope(pa[:, MLA_Q_RANK + MLA_KV_RANK:], cm, s1m, s2m, hm)
    q = _dot(cqn, wq_ref[...])
    kn = _dot(ckvn, wk_ref[...])
    vv = _dot(ckvn, wv_ref[...])
    for h in range(MLA_HEADS):
        sl = slice(h * LANE, (h + 1) * LANE)
        qm_ref[0, h] = (_rope(q[:, sl], cm, s1m, s2m, hm) * MLA_SCALE).astype(BF)
        km_ref[0, h] = (kn[:, sl] + kpe).astype(BF)
        vm_ref[0, h] = (vv[:, sl] + one_col).astype(BF)

    pn = _dot(xb, wn_ref[...])
    for h in range(NSA_HEADS):
        sl = slice(h * LANE, (h + 1) * LANE)
        nq_ref[0, h] = (_rope(pn[:, sl], cn, s1n, s2n, hn) * NSA_SCALE).astype(BF)
    o = NSA_HEADS * LANE
    kc_ref[0] = _rope(pn[:, o:o + LANE], cn, s1n, s2n, hn)
    vc_ref[0] = pn[:, o + LANE:o + 2 * LANE]
    o += 2 * LANE
    for g in range(NSA_KV_HEADS):
        def slab(i):
            return pn[:, o + (i * NSA_KV_HEADS + g) * LANE:o + (i * NSA_KV_HEADS + g + 1) * LANE]
        ks_ref[0, g] = (_rope(slab(0), cn, s1n, s2n, hn) + blk_onehot).astype(BF)
        vs_ref[0, g] = (slab(1) + one_col).astype(BF)
        kw_ref[0, g] = _rope(slab(2), cn, s1n, s2n, hn).astype(BF)
        vw_ref[0, g] = (slab(3) + one_col).astype(BF)
        gn_ref[0, g] = jax.nn.sigmoid(slab(4))


def _inproj_call(x, w, tabs):
    B, T, D = x.shape
    tm = min(TM_ROWS, T)
    H, G = MLA_HEADS, NSA_KV_HEADS
    head_spec = pl.BlockSpec((1, H, tm, LANE), lambda b, i: (b, 0, i, 0))
    grp_spec = pl.BlockSpec((1, G, tm, LANE), lambda b, i: (b, 0, i, 0))
    row_spec = pl.BlockSpec((1, tm, LANE), lambda b, i: (b, i, 0))
    tab_spec = pl.BlockSpec((tm, LANE), lambda b, i: (i, 0))
    head_shape = jax.ShapeDtypeStruct((B, H, T, LANE), BF)
    grp_shape = jax.ShapeDtypeStruct((B, G, T, LANE), BF)
    consts = [w["wa"], w["wn"], w["wq"], w["wk"], w["wv"], w["gq"], w["gkv"]]
    return pl.pallas_call(
        _inproj_kernel,
        grid=(B, T // tm),
        in_specs=[pl.BlockSpec((1, tm, D), lambda b, i: (b, i, 0))]
                 + [_const_spec(c.shape) for c in consts] + [tab_spec] * 6,
        out_specs=[head_spec] * 4 + [row_spec] * 2 + [grp_spec] * 4 + [grp_spec],
        out_shape=[head_shape] * 4 + [jax.ShapeDtypeStruct((B, T, LANE), F32)] * 2
                  + [grp_shape] * 4 + [jax.ShapeDtypeStruct((B, G, T, LANE), F32)],
        compiler_params=_params("parallel", "parallel"),
        name="in_proj",
    )(x, *consts, *tabs)


def _gelu_tanh(x):
    return 0.5 * x * (1.0 + jnp.tanh(np.sqrt(2.0 / np.pi) * (x + 0.044715 * (x * x * x))))


def _compress_kernel(kc_ref, vc_ref, pek_ref, pev_ref, w1k_ref, w1v_ref, b1k_ref, b1v_ref,
                     w2k_ref, w2v_ref, kco_ref, vco_ref):
    nc = kc_ref.shape[1]
    lane = lax.broadcasted_iota(jnp.int32, (nc, LANE), 1)
    one_col = (lane == ONE_LANE).astype(F32)
    for src, pe, w1, b1, w2, dst, extra in (
            (kc_ref, pek_ref, w1k_ref, b1k_ref, w2k_ref, kco_ref, None),
            (vc_ref, pev_ref, w1v_ref, b1v_ref, w2v_ref, vco_ref, one_col)):
        r = src[0]
        a_lo = (r + pe[0:1]).astype(BF)
        a_hi = (r + pe[1:2]).astype(BF)
        for g in range(NSA_KV_HEADS):
            h_hi = _dot(a_hi, w1[g, 1])
            h = _dot(a_lo, w1[g, 0]) + pltpu.roll(h_hi, nc - 1, 0) + b1[...]
            out = _dot(_gelu_tanh(h).astype(BF), w2[...])
            if extra is not None:
                out = out + extra
            dst[0, g] = out.astype(BF)


def _compress_call(kc, vc, w):
    B, T, _ = kc.shape
    nc = T // S_CMP
    width = S_CMP * LANE
    kc2 = kc.reshape(B, nc, width)
    vc2 = vc.reshape(B, nc, width)
    consts = [w["pek"], w["pev"], w["w1k"], w["w1v"], w["b1k"], w["b1v"], w["w2k"], w["w2v"]]
    in_spec = pl.BlockSpec((1, nc, width), lambda b: (b, 0, 0))
    out_spec = pl.BlockSpec((1, NSA_KV_HEADS, nc, LANE), lambda b: (b, 0, 0, 0))
    out_shape = jax.ShapeDtypeStruct((B, NSA_KV_HEADS, nc, LANE), BF)
    return pl.pallas_call(
        _compress_kernel,
        grid=(B,),
        in_specs=[in_spec, in_spec] + [_const_spec(c.shape) for c in consts],
        out_specs=[out_spec, out_spec],
        out_shape=[out_shape, out_shape],
        compiler_params=_params("parallel"),
        name="nsa_compress",
    )(kc2, vc2, *consts)


def _cmp_select_kernel(q_ref, kc_ref, vc_ref, ovt_ref, eye_ref, oc_ref, sel_ref, *, n_sel):
    tq = q_ref.shape[2]
    nc = kc_ref.shape[2]
    ns = ovt_ref.shape[0]
    qs = pl.program_id(2) * tq
    kc = kc_ref[0, 0]
    vc = vc_ref[0, 0]

    t_r = lax.broadcasted_iota(jnp.int32, (tq, nc), 0) + qs
    c_r = lax.broadcasted_iota(jnp.int32, (tq, nc), 1)
    valid = c_r * S_CMP + (L_CMP - 1) <= t_r
    t_c = lax.broadcasted_iota(jnp.int32, (nc, tq), 1) + qs
    c_c = lax.broadcasted_iota(jnp.int32, (nc, tq), 0)
    valid_t = c_c * S_CMP + (L_CMP - 1) <= t_c
    p_sum = jnp.zeros((nc, tq), F32)
    for h in range(NSA_HPG):
        q = q_ref[0, h]
        s = jnp.where(valid, _dot_nt(q, kc), NEG)
        e = jnp.where(valid, jnp.exp(s - s.max(-1, keepdims=True)), 0.0)
        acc = _dot(e.astype(BF), vc)
        l = acc[:, ONE_LANE:ONE_LANE + 1]
        oc_ref[0, h] = acc[:, :NSA_HD] / jnp.where(l > 0.0, l, 1.0)

        st = jnp.where(valid_t, _dot_nt(kc, q), NEG)
        et = jnp.where(valid_t, jnp.exp(st - st.max(0, keepdims=True)), 0.0)
        lt = et.sum(0, keepdims=True)
        p_sum = p_sum + et / jnp.where(lt > 0.0, lt, 1.0)

    hi = p_sum.astype(BF)
    lo = (p_sum - hi.astype(F32)).astype(BF)
    imp = _dot(ovt_ref[...], hi) + _dot(ovt_ref[...], lo)
    j = lax.broadcasted_iota(jnp.int32, (ns, tq), 0)
    cur = (lax.broadcasted_iota(jnp.int32, (ns, tq), 1) + qs) >> LOG2_L_SLC
    forced = (j == 0) | (j == cur) | (j == cur - 1)
    imp = jnp.where(forced, BIG, jnp.where(j <= cur, imp, NEG))
    cnt = jnp.zeros((ns, tq), F32)
    for k in range(ns):
        row_k = imp[k:k + 1, :]
        cnt = cnt + jnp.where(j > k, jnp.where(row_k >= imp, 1.0, 0.0), jnp.where(row_k > imp, 1.0, 0.0))
    not_sel = jnp.where(cnt < n_sel, 0.0, 1.0)
    pieces = [jnp.zeros((ONE_LANE, tq), F32), not_sel]
    if ns < LANE - ONE_LANE:
        pieces.append(jnp.zeros((LANE - ONE_LANE - ns, tq), F32))
    y = jnp.concatenate(pieces, axis=0).astype(BF)
    sel_ref[0, 0] = (_dot_nt(eye_ref[...], y) * NEG).astype(BF)


def _cmp_select_call(nq, kcc, vcc, ovt, eye, n_sel):
    B, H, T, _ = nq.shape
    G = NSA_KV_HEADS
    nc = kcc.shape[2]
    tq = min(TQ_CMP, T)
    return pl.pallas_call(
        functools.partial(_cmp_select_kernel, n_sel=n_sel),
        grid=(B, G, T // tq),
        in_specs=[pl.BlockSpec((1, NSA_HPG, tq, LANE), lambda b, g, i: (b, g, i, 0)),
                  pl.BlockSpec((1, 1, nc, LANE), lambda b, g, i: (b, g, 0, 0)),
                  pl.BlockSpec((1, 1, nc, LANE), lambda b, g, i: (b, g, 0, 0)),
                  _const_spec(ovt.shape), _const_spec(eye.shape)],
        out_specs=[pl.BlockSpec((1, NSA_HPG, tq, NSA_HD), lambda b, g, i: (b, g, i, 0)),
                   pl.BlockSpec((1, 1, tq, LANE), lambda b, g, i: (b, g, i, 0))],
        out_shape=[jax.ShapeDtypeStruct((B, H, T, NSA_HD), F32),
                   jax.ShapeDtypeStruct((B, G, T, LANE), BF)],
        compiler_params=_params("parallel", "parallel", "parallel"),
        name="nsa_cmp_select",
    )(nq, kcc, vcc, ovt, eye)


def _flash_step(q, k, v, m, acc, mask=None):
    s = _dot_nt(q, k)
    if mask is not None:
        s = jnp.where(mask, s, NEG)
    m_new = jnp.maximum(m, s.max(-1, keepdims=True))
    p = jnp.exp(s - m_new)
    acc = acc * jnp.exp(m - m_new) + _dot(p.astype(BF), v)
    return m_new, acc


def _mla_kernel(q_ref, k_ref, v_ref, o_ref):
    tq = q_ref.shape[2]
    qi = pl.program_id(2)
    causal = (lax.broadcasted_iota(jnp.int32, (tq, tq), 1)
              <= lax.broadcasted_iota(jnp.int32, (tq, tq), 0))
    outs = []
    for hh in range(2):
        q = q_ref[0, hh]

        def body(j, carry):
            off = pl.multiple_of(j * tq, tq)
            return _flash_step(q, k_ref[0, hh, pl.ds(off, tq), :], v_ref[0, hh, pl.ds(off, tq), :], *carry)

        carry = (jnp.full((tq, 1), NEG, F32), jnp.zeros((tq, LANE), F32))
        m, acc = lax.fori_loop(0, qi, body, carry)
        off = pl.multiple_of(qi * tq, tq)
        m, acc = _flash_step(q, k_ref[0, hh, pl.ds(off, tq), :], v_ref[0, hh, pl.ds(off, tq), :],
                             m, acc, causal)
        outs.append(acc[:, :MLA_V] / acc[:, ONE_LANE:ONE_LANE + 1])
    o_ref[0] = jnp.concatenate(outs, axis=1).astype(o_ref.dtype)


def _mla_call(qm, km, vm):
    B, H, T, _ = qm.shape
    tq = min(TQ_MLA, T)
    kv_spec = pl.BlockSpec((1, 2, T, LANE), lambda b, h, i: (b, h, 0, 0))
    return pl.pallas_call(
        _mla_kernel,
        grid=(B, H // 2, T // tq),
        in_specs=[pl.BlockSpec((1, 2, tq, LANE), lambda b, h, i: (b, h, i, 0)), kv_spec, kv_spec],
        out_specs=pl.BlockSpec((1, tq, 2 * MLA_V), lambda b, h, i: (b, i, h)),
        out_shape=jax.ShapeDtypeStruct((B, T, H * MLA_V), BF),
        compiler_params=_params("parallel", "parallel", "parallel"),
        name="mla_attention",
    )(qm, km, vm)


def _nsa_kernel(q_ref, sel_ref, ks_ref, vs_ref, kw_ref, vw_ref, oc_ref, gn_ref, o_ref, *, tk):
    tq = q_ref.shape[2]
    rows = NSA_HPG * tq
    qs = pl.program_id(2) * tq
    qa = (q_ref[0] + sel_ref[0, 0][None]).reshape(rows, LANE)
    qpos = qs + (lax.broadcasted_iota(jnp.int32, (rows, 1), 0) & (tq - 1))

    def body(j, carry):
        off = pl.multiple_of(j * tk, tk)
        return _flash_step(qa, ks_ref[0, 0, pl.ds(off, tk), :], vs_ref[0, 0, pl.ds(off, tk), :], *carry)

    jd = qs // tk
    carry = (jnp.full((rows, 1), NEG, F32), jnp.zeros((rows, LANE), F32))
    m, acc = lax.fori_loop(0, jd, body, carry)
    off = pl.multiple_of(jd * tk, tk)
    kpos = off + lax.broadcasted_iota(jnp.int32, (rows, tk), 1)
    m, acc = _flash_step(qa, ks_ref[0, 0, pl.ds(off, tk), :], vs_ref[0, 0, pl.ds(off, tk), :],
                         m, acc, kpos <= qpos)
    o_s = acc[:, :NSA_HD] / acc[:, ONE_LANE:ONE_LANE + 1]

    span = WINDOW + tq
    ws = pl.multiple_of(jnp.maximum(qs - WINDOW, 0), tq)
    kpos_w = ws + lax.broadcasted_iota(jnp.int32, (rows, span), 1)
    in_win = (kpos_w <= qpos) & (kpos_w > qpos - WINDOW)
    s_w = jnp.where(in_win, _dot_nt(qa, kw_ref[0, 0, pl.ds(ws, span), :]), NEG)
    p_w = jnp.exp(s_w - s_w.max(-1, keepdims=True))
    acc_w = _dot(p_w.astype(BF), vw_ref[0, 0, pl.ds(ws, span), :])
    o_w = acc_w[:, :NSA_HD] / acc_w[:, ONE_LANE:ONE_LANE + 1]

    gn = gn_ref[0, 0]
    outs = []
    for h in range(NSA_HPG):
        r = slice(h * tq, (h + 1) * tq)
        outs.append(gn[:, 3 * h:3 * h + 1] * oc_ref[0, h]
                    + gn[:, 3 * h + 1:3 * h + 2] * o_s[r]
                    + gn[:, 3 * h + 2:3 * h + 3] * o_w[r])
    o_ref[0] = jnp.concatenate(outs, axis=1).astype(o_ref.dtype)


def _nsa_call(nq, sel, ks, vs, kw, vw, oc, gn):
    B, H, T, _ = nq.shape
    G = NSA_KV_HEADS
    tq = min(TQ_NSA, T)
    tk = min(TK_NSA, T)
    assert tk % tq == 0 and WINDOW % tq == 0 and T >= WINDOW + tq
    kv_spec = pl.BlockSpec((1, 1, T, LANE), lambda b, g, i: (b, g, 0, 0))
    grp_spec = pl.BlockSpec((1, 1, tq, LANE), lambda b, g, i: (b, g, i, 0))
    return pl.pallas_call(
        functools.partial(_nsa_kernel, tk=tk),
        grid=(B, G, T // tq),
        in_specs=[pl.BlockSpec((1, NSA_HPG, tq, LANE), lambda b, g, i: (b, g, i, 0)),
                  grp_spec, kv_spec, kv_spec, kv_spec, kv_spec,
                  pl.BlockSpec((1, NSA_HPG, tq, NSA_HD), lambda b, g, i: (b, g, i, 0)),
                  grp_spec],
        out_specs=pl.BlockSpec((1, tq, NSA_HPG * NSA_HD), lambda b, g, i: (b, i, g)),
        out_shape=jax.ShapeDtypeStruct((B, T, H * NSA_HD), BF),
        compiler_params=_params("parallel", "parallel", "parallel"),
        name="nsa_select_window",
    )(nq, sel, ks, vs, kw, vw, oc, gn)


def _merge_kernel(x_ref, om_ref, on_ref, wgm_ref, wgn_ref, wpm_ref, wpn_ref, wo_ref, g_ref, b_ref, o_ref):
    x = x_ref[...]
    xb = x.astype(BF)
    y = (jax.nn.sigmoid(_dot(xb, wgm_ref[...])) * _dot(om_ref[...], wpm_ref[...])
         + jax.nn.sigmoid(_dot(xb, wgn_ref[...])) * _dot(on_ref[...], wpn_ref[...]))
    mix = _dot(y.astype(BF), wo_ref[...])
    o_ref[...] = _ln(ALPHA * x + mix, g_ref[...], b_ref[...])


def _merge_call(x2d, om, on, w):
    n, d = x2d.shape
    tm = min(TM_ROWS, n)
    consts = [w["wgm"], w["wgn"], w["wpm"], w["wpn"], w["wo"], w["ln_mix_g"], w["ln_mix_b"]]
    return pl.pallas_call(
        _merge_kernel,
        grid=(n // tm,),
        in_specs=[pl.BlockSpec((tm, d), lambda i: (i, 0)),
                  pl.BlockSpec((tm, om.shape[1]), lambda i: (i, 0)),
                  pl.BlockSpec((tm, on.shape[1]), lambda i: (i, 0))]
                 + [_const_spec(c.shape) for c in consts],
        out_specs=pl.BlockSpec((tm, d), lambda i: (i, 0)),
        out_shape=jax.ShapeDtypeStruct((n, d), F32),
        compiler_params=_params("parallel"),
        name="merge_out_ln",
    )(x2d, om, on, *consts)


def _pad_cols(w, width):
    return jnp.pad(w, ((0, 0), (0, width - w.shape[1])))


def _layer_weights(l, p):
    w_in = p["w_in"][l]
    d = w_in.shape[0]
    z64 = jnp.zeros((d, 64), F32)
    w = {}
    w["wa"] = jnp.concatenate(
        [w_in[:, OFF_CQ:OFF_KPE], z64, w_in[:, OFF_KPE:OFF_NQ], jnp.zeros((d, 32), F32)], axis=1).astype(BF)
    slabs = [_pad_cols(w_in[:, OFF_NQ + h * NSA_HD:OFF_NQ + (h + 1) * NSA_HD], LANE) for h in range(NSA_HEADS)]
    slabs += [w_in[:, OFF_NKV:OFF_NKV + LANE], w_in[:, OFF_NKV + LANE:OFF_NKV + 2 * LANE]]
    for i in range(2, 6):
        for g in range(NSA_KV_HEADS):
            c0 = OFF_NKV + i * LANE + g * NSA_HD
            slabs.append(_pad_cols(w_in[:, c0:c0 + NSA_HD], LANE))
    for g in range(NSA_KV_HEADS):
        c0 = OFF_GN + g * NSA_HPG * 3
        slabs.append(_pad_cols(w_in[:, c0:c0 + NSA_HPG * 3], LANE))
    w["wn"] = jnp.concatenate(slabs, axis=1).astype(BF)

    w_uq = p["w_uq"][l].reshape(MLA_Q_RANK, MLA_HEADS, MLA_NOPE + MLA_ROPE)
    w["wq"] = jnp.pad(w_uq, ((0, 0), (0, 0), (0, LANE - MLA_NOPE - MLA_ROPE))).reshape(MLA_Q_RANK, -1).astype(BF)
    w_ukv = p["w_ukv"][l].reshape(MLA_KV_RANK, MLA_HEADS, MLA_NOPE + MLA_V)
    w["wk"] = jnp.pad(w_ukv[:, :, :MLA_NOPE], ((0, 0), (0, 0), (0, LANE - MLA_NOPE))).reshape(MLA_KV_RANK, -1).astype(BF)
    w["wv"] = jnp.pad(w_ukv[:, :, MLA_NOPE:], ((0, 0), (0, 0), (0, LANE - MLA_V))).reshape(MLA_KV_RANK, -1).astype(BF)
    w["gq"] = p["q_norm_g"][l][None]
    w["gkv"] = p["kv_norm_g"][l][None]

    for nm, pe, w1, b1, w2 in (("k", "cmp_pe_k", "cmp_k_w1", "cmp_k_b1", "cmp_k_w2"),
                               ("v", "cmp_pe_v", "cmp_v_w1", "cmp_v_b1", "cmp_v_w2")):
        pe_l = p[pe][l].reshape(2, S_CMP, 1, NSA_HD)
        w["pe" + nm] = jnp.broadcast_to(pe_l, (2, S_CMP, NSA_KV_HEADS, NSA_HD)).reshape(2, S_CMP * LANE)
        w1_l = p[w1][l].reshape(2, S_CMP, 1, NSA_HD, CMP_HIDDEN)
        per_g = []
        for g in range(NSA_KV_HEADS):
            mask = (jnp.arange(NSA_KV_HEADS) == g).astype(F32).reshape(1, 1, NSA_KV_HEADS, 1, 1)
            per_g.append((w1_l * mask).reshape(2, S_CMP * LANE, CMP_HIDDEN))
        w["w1" + nm] = jnp.stack(per_g).astype(BF)
        w["b1" + nm] = p[b1][l][None]
        w["w2" + nm] = _pad_cols(p[w2][l], LANE).astype(BF)

    w["wgm"] = w_in[:, OFF_GM:OFF_GNS].astype(BF)
    w["wgn"] = w_in[:, OFF_GNS:].astype(BF)
    w["wpm"] = p["w_proj_mla"][l].astype(BF)
    w["wpn"] = p["w_proj_nsa"][l].astype(BF)
    w["wo"] = p["w_out"][l].astype(BF)
    w["ln_mix_g"] = p["ln_mix_g"][l][None]
    w["ln_mix_b"] = p["ln_mix_b"][l][None]
    for nm in ("ffn1", "ffn2"):
        for s in ("wg", "wu", "wd"):
            w[nm + s] = p[nm + "_" + s][l].astype(BF)
    for nm in ("ln_f1", "ln_f2"):
        w[nm + "_g"] = p[nm + "_g"][l][None]
        w[nm + "_b"] = p[nm + "_b"][l][None]
    return w


def _rope_tables(T):
    def tables(rot_dim):
        inv = 1.0 / (ROPE_THETA ** (jnp.arange(0, rot_dim, 2, dtype=F32) / rot_dim))
        ang = jnp.arange(T, dtype=F32)[:, None] * inv[None, :]
        return jnp.cos(ang), jnp.sin(ang)

    cos_m, sin_m = tables(MLA_ROPE)
    cos_n, sin_n = tables(NSA_ROT)
    one = lambda n: jnp.ones((T, n), F32)
    zero = lambda n: jnp.zeros((T, n), F32)
    hm, hn = MLA_ROPE // 2, NSA_ROT // 2
    rest_m = LANE - MLA_NOPE - MLA_ROPE
    cm = jnp.concatenate([one(MLA_NOPE), cos_m, cos_m, one(rest_m)], 1)
    s1m = jnp.concatenate([zero(MLA_NOPE), -sin_m, zero(hm), zero(rest_m)], 1)
    s2m = jnp.concatenate([zero(MLA_NOPE), zero(hm), sin_m, zero(rest_m)], 1)
    rest_n = NSA_HD - NSA_ROT
    cn = jnp.tile(jnp.concatenate([cos_n, cos_n, one(rest_n)], 1), (1, 2))
    s1n = jnp.tile(jnp.concatenate([-sin_n, zero(hn), zero(rest_n)], 1), (1, 2))
    s2n = jnp.tile(jnp.concatenate([zero(hn), sin_n, zero(rest_n)], 1), (1, 2))
    return cm, s1m, s2m, cn, s1n, s2n


def _overlap_t(T):
    nc, ns = T // S_CMP, T // L_SLC
    n_cmp = (T - L_CMP) // S_CMP + 1
    start = np.arange(nc) * S_CMP
    blk = np.arange(ns)
    ov = ((start[None, :] < (blk[:, None] + 1) * L_SLC) & (start[None, :] + L_CMP > blk[:, None] * L_SLC)
          & (np.arange(nc)[None, :] < n_cmp))
    return jnp.asarray(ov.astype(np.float32), dtype=BF)


def kernel(x, ln_f1_g, ln_f1_b, ffn1_wg, ffn1_wu, ffn1_wd, w_in, q_norm_g, w_uq, kv_norm_g, w_ukv,
           cmp_pe_k, cmp_k_w1, cmp_k_b1, cmp_k_w2, cmp_pe_v, cmp_v_w1, cmp_v_b1, cmp_v_w2,
           w_proj_mla, w_proj_nsa, w_out, ln_mix_g, ln_mix_b,
           ffn2_wg, ffn2_wu, ffn2_wd, ln_f2_g, ln_f2_b):
    p = dict(ln_f1_g=ln_f1_g, ln_f1_b=ln_f1_b, ffn1_wg=ffn1_wg, ffn1_wu=ffn1_wu, ffn1_wd=ffn1_wd,
             w_in=w_in, q_norm_g=q_norm_g, w_uq=w_uq, kv_norm_g=kv_norm_g, w_ukv=w_ukv,
             cmp_pe_k=cmp_pe_k, cmp_k_w1=cmp_k_w1, cmp_k_b1=cmp_k_b1, cmp_k_w2=cmp_k_w2,
             cmp_pe_v=cmp_pe_v, cmp_v_w1=cmp_v_w1, cmp_v_b1=cmp_v_b1, cmp_v_w2=cmp_v_w2,
             w_proj_mla=w_proj_mla, w_proj_nsa=w_proj_nsa, w_out=w_out,
             ln_mix_g=ln_mix_g, ln_mix_b=ln_mix_b, ffn2_wg=ffn2_wg, ffn2_wu=ffn2_wu, ffn2_wd=ffn2_wd,
             ln_f2_g=ln_f2_g, ln_f2_b=ln_f2_b)
    B, T, D = x.shape
    assert T % L_SLC == 0 and T // L_SLC <= LANE - ONE_LANE
    tabs = _rope_tables(T)
    ovt = _overlap_t(T)
    eye = jnp.eye(min(TQ_CMP, T), dtype=BF)
    n_sel = min(N_SEL, T // L_SLC)
    h = x.reshape(B * T, D)
    for l in range(DEPTH):
        w = _layer_weights(l, p)
        h = _ffn_call(h, w["ffn1wg"], w["ffn1wu"], w["ffn1wd"], w["ln_f1_g"], w["ln_f1_b"])
        qm, km, vm, nq, kc, vc, ks, vs, kw, vw, gn = _inproj_call(h.reshape(B, T, D), w, tabs)
        kcc, vcc = _compress_call(kc, vc, w)
        oc, sel = _cmp_select_call(nq, kcc, vcc, ovt, eye, n_sel)
        om = _mla_call(qm, km, vm)
        on = _nsa_call(nq, sel, ks, vs, kw, vw, oc, gn)
        h = _merge_call(h, om.reshape(B * T, -1), on.reshape(B * T, -1), w)
        h = _ffn_call(h, w["ffn2wg"], w["ffn2wu"], w["ffn2wd"], w["ln_f2_g"], w["ln_f2_b"])
    return h.reshape(B, T, D)
```

```python
import functools

import numpy as np
import jax
import jax.numpy as jnp
from jax import lax
from jax.experimental import pallas as pl
from jax.experimental.pallas import tpu as pltpu

D_MODEL = 1024
DEPTH = 2
MLA_HEADS = 8
MLA_NOPE = 64
MLA_ROPE = 32
MLA_V = 64
MLA_Q_RANK = 256
MLA_KV_RANK = 128
NSA_HEADS = 8
NSA_KV_HEADS = 2
NSA_HPG = NSA_HEADS // NSA_KV_HEADS
NSA_HD = 64
NSA_ROT = NSA_HD // 4
L_CMP = 32
S_CMP = 16
CMP_HIDDEN = 128
L_SLC = 64
LOG2_L_SLC = 6
N_SEL = 16
WINDOW = 512
ROPE_THETA = 500000.0
D_FF = 2816
ALPHA = (2 * DEPTH) ** 0.25
LN_EPS = 1e-5
RMS_EPS = 1e-6
NEG = -1e30
BIG = 1e30
LOG2E = 1.4426950408889634
MLA_SCALE = (MLA_NOPE + MLA_ROPE) ** -0.5 * LOG2E
NSA_SCALE = NSA_HD ** -0.5 * LOG2E

OFF_CQ = 0
OFF_CKV = OFF_CQ + MLA_Q_RANK
OFF_KPE = OFF_CKV + MLA_KV_RANK
OFF_NQ = OFF_KPE + MLA_ROPE
OFF_NKV = OFF_NQ + NSA_HEADS * NSA_HD
OFF_GN = OFF_NKV + 6 * NSA_KV_HEADS * NSA_HD
OFF_GM = OFF_GN + 3 * NSA_HEADS
OFF_GNS = OFF_GM + D_MODEL

LANE = 128
SUBLANE = 8
ONE_LANE = 64
VMEM_LIMIT = 56 * 1024 * 1024

BF = jnp.bfloat16
F32 = jnp.float32

FF_CHUNK = 1408
TM_ROWS = 512
TQ_MLA = 512
MLA_HEADS_PER_STEP = 2
TQ_CMP = 256
TQ_NSA = 256
TK_NSA = 512


def _dot(a, b):
    return jnp.dot(a, b, preferred_element_type=F32)


def _dot_nt(a, b):
    return lax.dot_general(a, b, (((1,), (1,)), ((), ())), preferred_element_type=F32)


def _ln(y, g, b):
    mu = jnp.mean(y, -1, keepdims=True)
    d = y - mu
    var = jnp.mean(d * d, -1, keepdims=True)
    return d * lax.rsqrt(var + LN_EPS) * g + b


def _params(*sem):
    return pltpu.CompilerParams(dimension_semantics=sem, vmem_limit_bytes=VMEM_LIMIT)


def _const_spec(shape):
    n = len(shape)
    return pl.BlockSpec(shape, lambda *_: (0,) * n)


def _ffn_ln(x, wg_ref, wu_ref, wd_ref, g, b):
    xb = x.astype(BF)
    acc = jnp.zeros(x.shape, F32)
    for c in range(0, D_FF, FF_CHUNK):
        hg = _dot(xb, wg_ref[:, c:c + FF_CHUNK])
        hu = _dot(xb, wu_ref[:, c:c + FF_CHUNK])
        h = (hg * jax.nn.sigmoid(hg)) * hu
        acc = acc + _dot(h.astype(BF), wd_ref[c:c + FF_CHUNK, :])
    return _ln(ALPHA * x + 0.5 * acc, g, b)


def _ffn_kernel(x_ref, wg_ref, wu_ref, wd_ref, g_ref, b_ref, o_ref):
    o_ref[...] = _ffn_ln(x_ref[...], wg_ref, wu_ref, wd_ref, g_ref[...], b_ref[...])


def _ffn_call(x2d, wg, wu, wd, g, b):
    n, d = x2d.shape
    tm = min(TM_ROWS, n)
    return pl.pallas_call(
        _ffn_kernel,
        grid=(n // tm,),
        in_specs=[pl.BlockSpec((tm, d), lambda i: (i, 0)),
                  _const_spec(wg.shape), _const_spec(wu.shape), _const_spec(wd.shape),
                  _const_spec(g.shape), _const_spec(b.shape)],
        out_specs=pl.BlockSpec((tm, d), lambda i: (i, 0)),
        out_shape=jax.ShapeDtypeStruct((n, d), F32),
        compiler_params=_params("parallel"),
        name="ffn_ln",
    )(x2d, wg, wu, wd, g, b)


def _rope(v, c, s1, s2, half):
    return v * c + pltpu.roll(v, LANE - half, 1) * s1 + pltpu.roll(v, half, 1) * s2


def _rms(v, g):
    return v * lax.rsqrt(jnp.mean(v * v, -1, keepdims=True) + RMS_EPS) * g


def _inproj_kernel(x_ref, wa_ref, wn_ref, wq_ref, wk_ref, wv_ref, gq_ref, gkv_ref,
                   cm_ref, s1m_ref, s2m_ref, cn_ref, s1n_ref, s2n_ref,
                   qm_ref, km_ref, vm_ref, nq_ref, kc_ref, vc_ref,
                   ks_ref, vs_ref, kw_ref, vw_ref, gn_ref):
    tm = x_ref.shape[1]
    t0 = pl.program_id(1) * tm
    xb = x_ref[0].astype(BF)
    lane = lax.broadcasted_iota(jnp.int32, (tm, LANE), 1)
    row = lax.broadcasted_iota(jnp.int32, (tm, LANE), 0) + t0
    one_col = (lane == ONE_LANE).astype(F32)
    blk_onehot = (lane - ONE_LANE == (row >> LOG2_L_SLC)).astype(F32)
    cm, s1m, s2m = cm_ref[...], s1m_ref[...], s2m_ref[...]
    cn, s1n, s2n = cn_ref[...], s1n_ref[...], s2n_ref[...]
    hm = MLA_ROPE // 2
    hn = NSA_ROT // 2

    pa = _dot(xb, wa_ref[...])
    cqn = _rms(pa[:, :MLA_Q_RANK], gq_ref[...]).astype(BF)
    ckvn = _rms(pa[:, MLA_Q_RANK:MLA_Q_RANK + MLA_KV_RANK], gkv_ref[...]).astype(BF)
    kpe = _rope(pa[:, MLA_Q_RANK + MLA_KV_RANK:], cm, s1m, s2m, hm)
    q = _dot(cqn, wq_ref[...])
    kn = _dot(ckvn, wk_ref[...])
    vv = _dot(ckvn, wv_ref[...])
    for h in range(MLA_HEADS):
        sl = slice(h * LANE, (h + 1) * LANE)
        qm_ref[0, h] = (_rope(q[:, sl], cm, s1m, s2m, hm) * MLA_SCALE).astype(BF)
        km_ref[0, h] = (kn[:, sl] + kpe).astype(BF)
        vm_ref[0, h] = (vv[:, sl] + one_col).astype(BF)

    pn = _dot(xb, wn_ref[...])
    for h in range(NSA_HEADS):
        sl = slice(h * LANE, (h + 1) * LANE)
        nq_ref[0, h] = (_rope(pn[:, sl], cn, s1n, s2n, hn) * NSA_SCALE).astype(BF)
    o = NSA_HEADS * LANE
    kc_ref[0] = _rope(pn[:, o:o + LANE], cn, s1n, s2n, hn)
    vc_ref[0] = pn[:, o + LANE:o + 2 * LANE]
    o += 2 * LANE
    for g in range(NSA_KV_HEADS):
        def slab(i):
            return pn[:, o + (i * NSA_KV_HEADS + g) * LANE:o + (i * NSA_KV_HEADS + g + 1) * LANE]
        ks_ref[0, g] = (_rope(slab(0), cn, s1n, s2n, hn) + blk_onehot).astype(BF)
        vs_ref[0, g] = (slab(1) + one_col).astype(BF)
        kw_ref[0, g] = _rope(slab(2), cn, s1n, s2n, hn).astype(BF)
        vw_ref[0, g] = (slab(3) + one_col).astype(BF)
        gn_ref[0, g] = jax.nn.sigmoid(slab(4))


def _inproj_call(x, w, tabs):
    B, T, D = x.shape
    tm = min(TM_ROWS, T)
    H, G = MLA_HEADS, NSA_KV_HEADS
    head_spec = pl.BlockSpec((1, H, tm, LANE), lambda b, i: (b, 0, i, 0))
    grp_spec = pl.BlockSpec((1, G, tm, LANE), lambda b, i: (b, 0, i, 0))
    row_spec = pl.BlockSpec((1, tm, LANE), lambda b, i: (b, i, 0))
    tab_spec = pl.BlockSpec((tm, LANE), lambda b, i: (i, 0))
    head_shape = jax.ShapeDtypeStruct((B, H, T, LANE), BF)
    grp_shape = jax.ShapeDtypeStruct((B, G, T, LANE), BF)
    consts = [w["wa"], w["wn"], w["wq"], w["wk"], w["wv"], w["gq"], w["gkv"]]
    return pl.pallas_call(
        _inproj_kernel,
        grid=(B, T // tm),
        in_specs=[pl.BlockSpec((1, tm, D), lambda b, i: (b, i, 0))]
                 + [_const_spec(c.shape) for c in consts] + [tab_spec] * 6,
        out_specs=[head_spec] * 4 + [row_spec] * 2 + [grp_spec] * 4 + [grp_spec],
        out_shape=[head_shape] * 4 + [jax.ShapeDtypeStruct((B, T, LANE), F32)] * 2
                  + [grp_shape] * 4 + [jax.ShapeDtypeStruct((B, G, T, LANE), F32)],
        compiler_params=_params("parallel", "parallel"),
        name="in_proj",
    )(x, *consts, *tabs)


def _gelu_tanh(x):
    return 0.5 * x * (1.0 + jnp.tanh(np.sqrt(2.0 / np.pi) * (x + 0.044715 * (x * x * x))))


def _compress_kernel(kc_ref, vc_ref, pek_ref, pev_ref, w1k_ref, w1v_ref, b1k_ref, b1v_ref,
                     w2k_ref, w2v_ref, kco_ref, vco_ref):
    nc = kc_ref.shape[1]
    lane = lax.broadcasted_iota(jnp.int32, (nc, LANE), 1)
    one_col = (lane == ONE_LANE).astype(F32)
    for src, pe, w1, b1, w2, dst, extra in (
            (kc_ref, pek_ref, w1k_ref, b1k_ref, w2k_ref, kco_ref, None),
            (vc_ref, pev_ref, w1v_ref, b1v_ref, w2v_ref, vco_ref, one_col)):
        r = src[0]
        a_lo = (r + pe[0:1]).astype(BF)
        a_hi = (r + pe[1:2]).astype(BF)
        for g in range(NSA_KV_HEADS):
            h_hi = _dot(a_hi, w1[g, 1])
            h = _dot(a_lo, w1[g, 0]) + pltpu.roll(h_hi, nc - 1, 0) + b1[...]
            out = _dot(_gelu_tanh(h).astype(BF), w2[...])
            if extra is not None:
                out = out + extra
            dst[0, g] = out.astype(BF)


def _compress_call(kc, vc, w):
    B, T, _ = kc.shape
    nc = T // S_CMP
    width = S_CMP * LANE
    kc2 = kc.reshape(B, nc, width)
    vc2 = vc.reshape(B, nc, width)
    consts = [w["pek"], w["pev"], w["w1k"], w["w1v"], w["b1k"], w["b1v"], w["w2k"], w["w2v"]]
    in_spec = pl.BlockSpec((1, nc, width), lambda b: (b, 0, 0))
    out_spec = pl.BlockSpec((1, NSA_KV_HEADS, nc, LANE), lambda b: (b, 0, 0, 0))
    out_shape = jax.ShapeDtypeStruct((B, NSA_KV_HEADS, nc, LANE), BF)
    return pl.pallas_call(
        _compress_kernel,
        grid=(B,),
        in_specs=[in_spec, in_spec] + [_const_spec(c.shape) for c in consts],
        out_specs=[out_spec, out_spec],
        out_shape=[out_shape, out_shape],
        compiler_params=_params("parallel"),
        name="nsa_compress",
    )(kc2, vc2, *consts)


def _cmp_select_kernel(q_ref, kc_ref, vc_ref, ovt_ref, eye_ref, oc_ref, sel_ref, *, n_sel):
    tq = q_ref.shape[2]
    nc = kc_ref.shape[2]
    ns = ovt_ref.shape[0]
    qs = pl.program_id(2) * tq
    kc = kc_ref[0, 0]
    vc = vc_ref[0, 0]

    t_r = lax.broadcasted_iota(jnp.int32, (tq, nc), 0) + qs
    c_r = lax.broadcasted_iota(jnp.int32, (tq, nc), 1)
    valid = c_r * S_CMP + (L_CMP - 1) <= t_r
    t_c = lax.broadcasted_iota(jnp.int32, (nc, tq), 1) + qs
    c_c = lax.broadcasted_iota(jnp.int32, (nc, tq), 0)
    valid_t = c_c * S_CMP + (L_CMP - 1) <= t_c
    p_sum = jnp.zeros((nc, tq), F32)
    for h in range(NSA_HPG):
        q = q_ref[0, h]
        s = jnp.where(valid, _dot_nt(q, kc), NEG)
        e = jnp.where(valid, jnp.exp2(s - s.max(-1, keepdims=True)), 0.0)
        acc = _dot(e.astype(BF), vc)
        l = acc[:, ONE_LANE:ONE_LANE + 1]
        oc_ref[0, h] = acc[:, :NSA_HD] / jnp.where(l > 0.0, l, 1.0)

        st = jnp.where(valid_t, _dot_nt(kc, q), NEG)
        et = jnp.where(valid_t, jnp.exp2(st - st.max(0, keepdims=True)), 0.0)
        lt = et.sum(0, keepdims=True)
        p_sum = p_sum + et / jnp.where(lt > 0.0, lt, 1.0)

    hi = p_sum.astype(BF)
    lo = (p_sum - hi.astype(F32)).astype(BF)
    imp = _dot(ovt_ref[...], hi) + _dot(ovt_ref[...], lo)
    j = lax.broadcasted_iota(jnp.int32, (ns, tq), 0)
    cur = (lax.broadcasted_iota(jnp.int32, (ns, tq), 1) + qs) >> LOG2_L_SLC
    forced = (j == 0) | (j == cur) | (j == cur - 1)
    imp = jnp.where(forced, BIG, jnp.where(j <= cur, imp, NEG))
    groups = [imp[r * SUBLANE:(r + 1) * SUBLANE] for r in range(ns // SUBLANE)]
    cnts = [jnp.zeros((SUBLANE, tq), F32)] * len(groups)
    j_in = lax.broadcasted_iota(jnp.int32, (SUBLANE, tq), 0)
    for k in range(ns):
        row_k = imp[k:k + 1, :]
        for r, grp in enumerate(groups):
            if r > k // SUBLANE:
                ahead = jnp.where(row_k >= grp, 1.0, 0.0)
            elif r < k // SUBLANE:
                ahead = jnp.where(row_k > grp, 1.0, 0.0)
            else:
                ahead = jnp.where(j_in > k % SUBLANE, jnp.where(row_k >= grp, 1.0, 0.0),
                                  jnp.where(row_k > grp, 1.0, 0.0))
            cnts[r] = cnts[r] + ahead
    cnt = jnp.concatenate(cnts, axis=0)
    not_sel = jnp.where(cnt < n_sel, 0.0, 1.0)
    pieces = [jnp.zeros((ONE_LANE, tq), F32), not_sel]
    if ns < LANE - ONE_LANE:
        pieces.append(jnp.zeros((LANE - ONE_LANE - ns, tq), F32))
    y = jnp.concatenate(pieces, axis=0).astype(BF)
    sel_ref[0, 0] = (_dot_nt(eye_ref[...], y) * NEG).astype(BF)


def _cmp_select_call(nq, kcc, vcc, ovt, eye, n_sel):
    B, H, T, _ = nq.shape
    G = NSA_KV_HEADS
    nc = kcc.shape[2]
    tq = min(TQ_CMP, T)
    return pl.pallas_call(
        functools.partial(_cmp_select_kernel, n_sel=n_sel),
        grid=(B, G, T // tq),
        in_specs=[pl.BlockSpec((1, NSA_HPG, tq, LANE), lambda b, g, i: (b, g, i, 0)),
                  pl.BlockSpec((1, 1, nc, LANE), lambda b, g, i: (b, g, 0, 0)),
                  pl.BlockSpec((1, 1, nc, LANE), lambda b, g, i: (b, g, 0, 0)),
                  _const_spec(ovt.shape), _const_spec(eye.shape)],
        out_specs=[pl.BlockSpec((1, NSA_HPG, tq, NSA_HD), lambda b, g, i: (b, g, i, 0)),
                   pl.BlockSpec((1, 1, tq, LANE), lambda b, g, i: (b, g, i, 0))],
        out_shape=[jax.ShapeDtypeStruct((B, H, T, NSA_HD), F32),
                   jax.ShapeDtypeStruct((B, G, T, LANE), BF)],
        compiler_params=_params("parallel", "parallel", "parallel"),
        name="nsa_cmp_select",
    )(nq, kcc, vcc, ovt, eye)


def _flash_step(q, k, v, m, acc, bias=None):
    s = _dot_nt(q, k)
    if bias is not None:
        reps = s.shape[0] // bias.shape[0]
        s = s + bias if reps == 1 else (s.reshape(reps, *bias.shape) + bias[None]).reshape(s.shape)
    m_new = jnp.maximum(m, s.max(-1, keepdims=True))
    p = jnp.exp2(s - m_new)
    acc = acc * jnp.exp2(m - m_new) + _dot(p.astype(BF), v)
    return m_new, acc


def _mla_kernel(q_ref, k_ref, v_ref, o_ref):
    nh, tq = q_ref.shape[1], q_ref.shape[2]
    qi = pl.program_id(2)
    causal_bias = jnp.where(lax.broadcasted_iota(jnp.int32, (tq, tq), 1)
                            <= lax.broadcasted_iota(jnp.int32, (tq, tq), 0), 0.0, NEG)

    def step(off, carry, bias):
        out = []
        for hh in range(nh):
            out += _flash_step(q_ref[0, hh], k_ref[0, hh, pl.ds(off, tq), :], v_ref[0, hh, pl.ds(off, tq), :],
                               carry[2 * hh], carry[2 * hh + 1], bias)
        return tuple(out)

    carry = (jnp.full((tq, 1), NEG, F32), jnp.zeros((tq, LANE), F32)) * nh
    carry = lax.fori_loop(0, qi, lambda j, c: step(pl.multiple_of(j * tq, tq), c, None), carry)
    carry = step(pl.multiple_of(qi * tq, tq), carry, causal_bias)
    outs = [carry[2 * hh + 1][:, :MLA_V] / carry[2 * hh + 1][:, ONE_LANE:ONE_LANE + 1] for hh in range(nh)]
    o_ref[0] = jnp.concatenate(outs, axis=1).astype(o_ref.dtype)


def _mla_call(qm, km, vm):
    B, H, T, _ = qm.shape
    tq = min(TQ_MLA, T)
    nh = MLA_HEADS_PER_STEP
    kv_spec = pl.BlockSpec((1, nh, T, LANE), lambda b, h, i: (b, h, 0, 0))
    return pl.pallas_call(
        _mla_kernel,
        grid=(B, H // nh, T // tq),
        in_specs=[pl.BlockSpec((1, nh, tq, LANE), lambda b, h, i: (b, h, i, 0)), kv_spec, kv_spec],
        out_specs=pl.BlockSpec((1, tq, nh * MLA_V), lambda b, h, i: (b, i, h)),
        out_shape=jax.ShapeDtypeStruct((B, T, H * MLA_V), BF),
        compiler_params=_params("parallel", "parallel", "parallel"),
        name="mla_attention",
    )(qm, km, vm)


def _nsa_kernel(q_ref, sel_ref, ks_ref, vs_ref, kw_ref, vw_ref, oc_ref, gn_ref, o_ref, *, tk):
    tq = q_ref.shape[2]
    rows = NSA_HPG * tq
    qs = pl.program_id(2) * tq
    qa = (q_ref[0] + sel_ref[0, 0][None]).reshape(rows, LANE)
    qpos = qs + lax.broadcasted_iota(jnp.int32, (tq, 1), 0)

    def body(j, carry):
        off = pl.multiple_of(j * tk, tk)
        return _flash_step(qa, ks_ref[0, 0, pl.ds(off, tk), :], vs_ref[0, 0, pl.ds(off, tk), :], *carry)

    jd = qs // tk
    carry = (jnp.full((rows, 1), NEG, F32), jnp.zeros((rows, LANE), F32))
    m, acc = lax.fori_loop(0, jd, body, carry)
    off = pl.multiple_of(jd * tk, tk)
    kpos = off + lax.broadcasted_iota(jnp.int32, (tq, tk), 1)
    m, acc = _flash_step(qa, ks_ref[0, 0, pl.ds(off, tk), :], vs_ref[0, 0, pl.ds(off, tk), :],
                         m, acc, jnp.where(kpos <= qpos, 0.0, NEG))
    o_s = acc[:, :NSA_HD] / acc[:, ONE_LANE:ONE_LANE + 1]

    span = WINDOW + tq
    ws = pl.multiple_of(jnp.maximum(qs - WINDOW, 0), tq)
    kpos_w = ws + lax.broadcasted_iota(jnp.int32, (tq, span), 1)
    bias_w = jnp.where(kpos_w <= qpos, jnp.where(kpos_w > qpos - WINDOW, 0.0, NEG), NEG)
    _, acc_w = _flash_step(qa, kw_ref[0, 0, pl.ds(ws, span), :], vw_ref[0, 0, pl.ds(ws, span), :],
                           jnp.full((rows, 1), NEG, F32), jnp.zeros((rows, LANE), F32), bias_w)
    o_w = acc_w[:, :NSA_HD] / acc_w[:, ONE_LANE:ONE_LANE + 1]

    gn = gn_ref[0, 0]
    outs = []
    for h in range(NSA_HPG):
        r = slice(h * tq, (h + 1) * tq)
        outs.append(gn[:, 3 * h:3 * h + 1] * oc_ref[0, h]
                    + gn[:, 3 * h + 1:3 * h + 2] * o_s[r]
                    + gn[:, 3 * h + 2:3 * h + 3] * o_w[r])
    o_ref[0] = jnp.concatenate(outs, axis=1).astype(o_ref.dtype)


def _nsa_call(nq, sel, ks, vs, kw, vw, oc, gn):
    B, H, T, _ = nq.shape
    G = NSA_KV_HEADS
    tq = min(TQ_NSA, T)
    tk = min(TK_NSA, T)
    assert tk % tq == 0 and WINDOW % tq == 0 and T >= WINDOW + tq
    kv_spec = pl.BlockSpec((1, 1, T, LANE), lambda b, g, i: (b, g, 0, 0))
    grp_spec = pl.BlockSpec((1, 1, tq, LANE), lambda b, g, i: (b, g, i, 0))
    return pl.pallas_call(
        functools.partial(_nsa_kernel, tk=tk),
        grid=(B, G, T // tq),
        in_specs=[pl.BlockSpec((1, NSA_HPG, tq, LANE), lambda b, g, i: (b, g, i, 0)),
                  grp_spec, kv_spec, kv_spec, kv_spec, kv_spec,
                  pl.BlockSpec((1, NSA_HPG, tq, NSA_HD), lambda b, g, i: (b, g, i, 0)),
                  grp_spec],
        out_specs=pl.BlockSpec((1, tq, NSA_HPG * NSA_HD), lambda b, g, i: (b, i, g)),
        out_shape=jax.ShapeDtypeStruct((B, T, H * NSA_HD), BF),
        compiler_params=_params("parallel", "parallel", "parallel"),
        name="nsa_select_window",
    )(nq, sel, ks, vs, kw, vw, oc, gn)


def _merge_kernel(x_ref, om_ref, on_ref, wgm_ref, wgn_ref, wpm_ref, wpn_ref, wo_ref, g_ref, b_ref, o_ref):
    x = x_ref[...]
    xb = x.astype(BF)
    y = (jax.nn.sigmoid(_dot(xb, wgm_ref[...])) * _dot(om_ref[...], wpm_ref[...])
         + jax.nn.sigmoid(_dot(xb, wgn_ref[...])) * _dot(on_ref[...], wpn_ref[...]))
    mix = _dot(y.astype(BF), wo_ref[...])
    o_ref[...] = _ln(ALPHA * x + mix, g_ref[...], b_ref[...])


def _merge_call(x2d, om, on, w):
    n, d = x2d.shape
    tm = min(TM_ROWS, n)
    consts = [w["wgm"], w["wgn"], w["wpm"], w["wpn"], w["wo"], w["ln_mix_g"], w["ln_mix_b"]]
    return pl.pallas_call(
        _merge_kernel,
        grid=(n // tm,),
        in_specs=[pl.BlockSpec((tm, d), lambda i: (i, 0)),
                  pl.BlockSpec((tm, om.shape[1]), lambda i: (i, 0)),
                  pl.BlockSpec((tm, on.shape[1]), lambda i: (i, 0))]
                 + [_const_spec(c.shape) for c in consts],
        out_specs=pl.BlockSpec((tm, d), lambda i: (i, 0)),
        out_shape=jax.ShapeDtypeStruct((n, d), F32),
        compiler_params=_params("parallel"),
        name="merge_out_ln",
    )(x2d, om, on, *consts)


def _pad_cols(w, width):
    return jnp.pad(w, ((0, 0), (0, width - w.shape[1])))


def _layer_weights(l, p):
    w_in = p["w_in"][l]
    d = w_in.shape[0]
    z64 = jnp.zeros((d, 64), F32)
    w = {}
    w["wa"] = jnp.concatenate(
        [w_in[:, OFF_CQ:OFF_KPE], z64, w_in[:, OFF_KPE:OFF_NQ], jnp.zeros((d, 32), F32)], axis=1).astype(BF)
    slabs = [_pad_cols(w_in[:, OFF_NQ + h * NSA_HD:OFF_NQ + (h + 1) * NSA_HD], LANE) for h in range(NSA_HEADS)]
    slabs += [w_in[:, OFF_NKV:OFF_NKV + LANE], w_in[:, OFF_NKV + LANE:OFF_NKV + 2 * LANE]]
    for i in range(2, 6):
        for g in range(NSA_KV_HEADS):
            c0 = OFF_NKV + i * LANE + g * NSA_HD
            slabs.append(_pad_cols(w_in[:, c0:c0 + NSA_HD], LANE))
    for g in range(NSA_KV_HEADS):
        c0 = OFF_GN + g * NSA_HPG * 3
        slabs.append(_pad_cols(w_in[:, c0:c0 + NSA_HPG * 3], LANE))
    w["wn"] = jnp.concatenate(slabs, axis=1).astype(BF)

    w_uq = p["w_uq"][l].reshape(MLA_Q_RANK, MLA_HEADS, MLA_NOPE + MLA_ROPE)
    w["wq"] = jnp.pad(w_uq, ((0, 0), (0, 0), (0, LANE - MLA_NOPE - MLA_ROPE))).reshape(MLA_Q_RANK, -1).astype(BF)
    w_ukv = p["w_ukv"][l].reshape(MLA_KV_RANK, MLA_HEADS, MLA_NOPE + MLA_V)
    w["wk"] = jnp.pad(w_ukv[:, :, :MLA_NOPE], ((0, 0), (0, 0), (0, LANE - MLA_NOPE))).reshape(MLA_KV_RANK, -1).astype(BF)
    w["wv"] = jnp.pad(w_ukv[:, :, MLA_NOPE:], ((0, 0), (0, 0), (0, LANE - MLA_V))).reshape(MLA_KV_RANK, -1).astype(BF)
    w["gq"] = p["q_norm_g"][l][None]
    w["gkv"] = p["kv_norm_g"][l][None]

    for nm, pe, w1, b1, w2 in (("k", "cmp_pe_k", "cmp_k_w1", "cmp_k_b1", "cmp_k_w2"),
                               ("v", "cmp_pe_v", "cmp_v_w1", "cmp_v_b1", "cmp_v_w2")):
        pe_l = p[pe][l].reshape(2, S_CMP, 1, NSA_HD)
        w["pe" + nm] = jnp.broadcast_to(pe_l, (2, S_CMP, NSA_KV_HEADS, NSA_HD)).reshape(2, S_CMP * LANE)
        w1_l = p[w1][l].reshape(2, S_CMP, 1, NSA_HD, CMP_HIDDEN)
        per_g = []
        for g in range(NSA_KV_HEADS):
            mask = (jnp.arange(NSA_KV_HEADS) == g).astype(F32).reshape(1, 1, NSA_KV_HEADS, 1, 1)
            per_g.append((w1_l * mask).reshape(2, S_CMP * LANE, CMP_HIDDEN))
        w["w1" + nm] = jnp.stack(per_g).astype(BF)
        w["b1" + nm] = p[b1][l][None]
        w["w2" + nm] = _pad_cols(p[w2][l], LANE).astype(BF)

    w["wgm"] = w_in[:, OFF_GM:OFF_GNS].astype(BF)
    w["wgn"] = w_in[:, OFF_GNS:].astype(BF)
    w["wpm"] = p["w_proj_mla"][l].astype(BF)
    w["wpn"] = p["w_proj_nsa"][l].astype(BF)
    w["wo"] = p["w_out"][l].astype(BF)
    w["ln_mix_g"] = p["ln_mix_g"][l][None]
    w["ln_mix_b"] = p["ln_mix_b"][l][None]
    for nm in ("ffn1", "ffn2"):
        for s in ("wg", "wu", "wd"):
            w[nm + s] = p[nm + "_" + s][l].astype(BF)
    for nm in ("ln_f1", "ln_f2"):
        w[nm + "_g"] = p[nm + "_g"][l][None]
        w[nm + "_b"] = p[nm + "_b"][l][None]
    return w


def _rope_tables(T):
    def tables(rot_dim):
        inv = 1.0 / (ROPE_THETA ** (jnp.arange(0, rot_dim, 2, dtype=F32) / rot_dim))
        ang = jnp.arange(T, dtype=F32)[:, None] * inv[None, :]
        return jnp.cos(ang), jnp.sin(ang)

    cos_m, sin_m = tables(MLA_ROPE)
    cos_n, sin_n = tables(NSA_ROT)
    one = lambda n: jnp.ones((T, n), F32)
    zero = lambda n: jnp.zeros((T, n), F32)
    hm, hn = MLA_ROPE // 2, NSA_ROT // 2
    rest_m = LANE - MLA_NOPE - MLA_ROPE
    cm = jnp.concatenate([one(MLA_NOPE), cos_m, cos_m, one(rest_m)], 1)
    s1m = jnp.concatenate([zero(MLA_NOPE), -sin_m, zero(hm), zero(rest_m)], 1)
    s2m = jnp.concatenate([zero(MLA_NOPE), zero(hm), sin_m, zero(rest_m)], 1)
    rest_n = NSA_HD - NSA_ROT
    cn = jnp.tile(jnp.concatenate([cos_n, cos_n, one(rest_n)], 1), (1, 2))
    s1n = jnp.tile(jnp.concatenate([-sin_n, zero(hn), zero(rest_n)], 1), (1, 2))
    s2n = jnp.tile(jnp.concatenate([zero(hn), sin_n, zero(rest_n)], 1), (1, 2))
    return cm, s1m, s2m, cn, s1n, s2n


def _overlap_t(T):
    nc, ns = T // S_CMP, T // L_SLC
    n_cmp = (T - L_CMP) // S_CMP + 1
    start = np.arange(nc) * S_CMP
    blk = np.arange(ns)
    ov = ((start[None, :] < (blk[:, None] + 1) * L_SLC) & (start[None, :] + L_CMP > blk[:, None] * L_SLC)
          & (np.arange(nc)[None, :] < n_cmp))
    return jnp.asarray(ov.astype(np.float32), dtype=BF)


def kernel(x, ln_f1_g, ln_f1_b, ffn1_wg, ffn1_wu, ffn1_wd, w_in, q_norm_g, w_uq, kv_norm_g, w_ukv,
           cmp_pe_k, cmp_k_w1, cmp_k_b1, cmp_k_w2, cmp_pe_v, cmp_v_w1, cmp_v_b1, cmp_v_w2,
           w_proj_mla, w_proj_nsa, w_out, ln_mix_g, ln_mix_b,
           ffn2_wg, ffn2_wu, ffn2_wd, ln_f2_g, ln_f2_b):
    p = dict(ln_f1_g=ln_f1_g, ln_f1_b=ln_f1_b, ffn1_wg=ffn1_wg, ffn1_wu=ffn1_wu, ffn1_wd=ffn1_wd,
             w_in=w_in, q_norm_g=q_norm_g, w_uq=w_uq, kv_norm_g=kv_norm_g, w_ukv=w_ukv,
             cmp_pe_k=cmp_pe_k, cmp_k_w1=cmp_k_w1, cmp_k_b1=cmp_k_b1, cmp_k_w2=cmp_k_w2,
             cmp_pe_v=cmp_pe_v, cmp_v_w1=cmp_v_w1, cmp_v_b1=cmp_v_b1, cmp_v_w2=cmp_v_w2,
             w_proj_mla=w_proj_mla, w_proj_nsa=w_proj_nsa, w_out=w_out,
             ln_mix_g=ln_mix_g, ln_mix_b=ln_mix_b, ffn2_wg=ffn2_wg, ffn2_wu=ffn2_wu, ffn2_wd=ffn2_wd,
             ln_f2_g=ln_f2_g, ln_f2_b=ln_f2_b)
    B, T, D = x.shape
    assert T % L_SLC == 0 and T // L_SLC <= LANE - ONE_LANE
    tabs = _rope_tables(T)
    ovt = _overlap_t(T)
    eye = jnp.eye(min(TQ_CMP, T), dtype=BF)
    n_sel = min(N_SEL, T // L_SLC)
    h = x.reshape(B * T, D)
    for l in range(DEPTH):
        w = _layer_weights(l, p)
        h = _ffn_call(h, w["ffn1wg"], w["ffn1wu"], w["ffn1wd"], w["ln_f1_g"], w["ln_f1_b"])
        qm, km, vm, nq, kc, vc, ks, vs, kw, vw, gn = _inproj_call(h.reshape(B, T, D), w, tabs)
        kcc, vcc = _compress_call(kc, vc, w)
        oc, sel = _cmp_select_call(nq, kcc, vcc, ovt, eye, n_sel)
        om = _mla_call(qm, km, vm)
        on = _nsa_call(nq, sel, ks, vs, kw, vw, oc, gn)
        h = _merge_call(h, om.reshape(B * T, -1), on.reshape(B * T, -1), w)
        h = _ffn_call(h, w["ffn2wg"], w["ffn2wu"], w["ffn2wd"], w["ln_f2_g"], w["ln_f2_b"])
    return h.reshape(B, T, D)
```

```python
import functools

import numpy as np
import jax
import jax.numpy as jnp
from jax import lax
from jax.experimental import pallas as pl
from jax.experimental.pallas import tpu as pltpu

D_MODEL = 1024
DEPTH = 2
MLA_HEADS = 8
MLA_NOPE = 64
MLA_ROPE = 32
MLA_V = 64
MLA_Q_RANK = 256
MLA_KV_RANK = 128
NSA_HEADS = 8
NSA_KV_HEADS = 2
NSA_HPG = NSA_HEADS // NSA_KV_HEADS
NSA_HD = 64
NSA_ROT = NSA_HD // 4
L_CMP = 32
S_CMP = 16
CMP_HIDDEN = 128
L_SLC = 64
LOG2_L_SLC = 6
N_SEL = 16
WINDOW = 512
ROPE_THETA = 500000.0
D_FF = 2816
ALPHA = (2 * DEPTH) ** 0.25
LN_EPS = 1e-5
RMS_EPS = 1e-6
NEG = -1e30
BIG = 1e30
LOG2E = 1.4426950408889634
MLA_SCALE = (MLA_NOPE + MLA_ROPE) ** -0.5 * LOG2E
NSA_SCALE = NSA_HD ** -0.5 * LOG2E

OFF_CQ = 0
OFF_CKV = OFF_CQ + MLA_Q_RANK
OFF_KPE = OFF_CKV + MLA_KV_RANK
OFF_NQ = OFF_KPE + MLA_ROPE
OFF_NKV = OFF_NQ + NSA_HEADS * NSA_HD
OFF_GN = OFF_NKV + 6 * NSA_KV_HEADS * NSA_HD
OFF_GM = OFF_GN + 3 * NSA_HEADS
OFF_GNS = OFF_GM + D_MODEL

LANE = 128
SUBLANE = 8
ONE_LANE = 64
VMEM_LIMIT = 56 * 1024 * 1024

BF = jnp.bfloat16
F32 = jnp.float32

MXU_DIM = 256
FF_CHUNKS = ((0, 6 * MXU_DIM), (6 * MXU_DIM, D_FF))
TM_ROWS = 512
TQ_MLA = 512
MLA_HEADS_PER_STEP = 4
TQ_CMP = 256
TQ_NSA = 256
TK_NSA = 512


def _dot(a, b):
    return jnp.dot(a, b, preferred_element_type=F32)


def _dot_nt(a, b):
    return lax.dot_general(a, b, (((1,), (1,)), ((), ())), preferred_element_type=F32)


def _ln(y, g, b):
    mu = jnp.mean(y, -1, keepdims=True)
    d = y - mu
    var = jnp.mean(d * d, -1, keepdims=True)
    return d * lax.rsqrt(var + LN_EPS) * g + b


def _params(*sem):
    return pltpu.CompilerParams(dimension_semantics=sem, vmem_limit_bytes=VMEM_LIMIT)


def _const_spec(shape):
    n = len(shape)
    return pl.BlockSpec(shape, lambda *_: (0,) * n)


def _ffn_ln(x, wg_ref, wu_ref, wd_ref, g, b):
    xb = x.astype(BF)
    acc = jnp.zeros(x.shape, F32)
    for c0, c1 in FF_CHUNKS:
        hg = _dot(xb, wg_ref[:, c0:c1])
        hu = _dot(xb, wu_ref[:, c0:c1])
        h = (hg * jax.nn.sigmoid(hg)) * hu
        acc = acc + _dot(h.astype(BF), wd_ref[c0:c1, :])
    return _ln(ALPHA * x + 0.5 * acc, g, b)


def _ffn_kernel(x_ref, wg_ref, wu_ref, wd_ref, g_ref, b_ref, o_ref):
    o_ref[...] = _ffn_ln(x_ref[...], wg_ref, wu_ref, wd_ref, g_ref[...], b_ref[...])


def _ffn_call(x2d, wg, wu, wd, g, b):
    n, d = x2d.shape
    tm = min(TM_ROWS, n)
    return pl.pallas_call(
        _ffn_kernel,
        grid=(n // tm,),
        in_specs=[pl.BlockSpec((tm, d), lambda i: (i, 0)),
                  _const_spec(wg.shape), _const_spec(wu.shape), _const_spec(wd.shape),
                  _const_spec(g.shape), _const_spec(b.shape)],
        out_specs=pl.BlockSpec((tm, d), lambda i: (i, 0)),
        out_shape=jax.ShapeDtypeStruct((n, d), F32),
        compiler_params=_params("parallel"),
        name="ffn_ln",
    )(x2d, wg, wu, wd, g, b)


def _rope(v, c, s1, s2, half):
    return v * c + pltpu.roll(v, LANE - half, 1) * s1 + pltpu.roll(v, half, 1) * s2


def _rms(v, g):
    return v * lax.rsqrt(jnp.mean(v * v, -1, keepdims=True) + RMS_EPS) * g


def _inproj_kernel(x_ref, wa_ref, wn_ref, wq_ref, wk_ref, wv_ref, gq_ref, gkv_ref,
                   cm_ref, s1m_ref, s2m_ref, cn_ref, s1n_ref, s2n_ref,
                   qm_ref, km_ref, vm_ref, nq_ref, kc_ref, vc_ref,
                   ks_ref, vs_ref, kw_ref, vw_ref, gn_ref):
    tm = x_ref.shape[1]
    t0 = pl.program_id(1) * tm
    xb = x_ref[0].astype(BF)
    lane = lax.broadcasted_iota(jnp.int32, (tm, LANE), 1)
    row = lax.broadcasted_iota(jnp.int32, (tm, LANE), 0) + t0
    one_col = (lane == ONE_LANE).astype(F32)
    blk_onehot = (lane - ONE_LANE == (row >> LOG2_L_SLC)).astype(F32)
    cm, s1m, s2m = cm_ref[...], s1m_ref[...], s2m_ref[...]
    cn, s1n, s2n = cn_ref[...], s1n_ref[...], s2n_ref[...]
    hm = MLA_ROPE // 2
    hn = NSA_ROT // 2

    pa = _dot(xb, wa_ref[...])
    cqn = _rms(pa[:, :MLA_Q_RANK], gq_ref[...]).astype(BF)
    ckvn = _rms(pa[:, MLA_Q_RANK:MLA_Q_RANK + MLA_KV_RANK], gkv_ref[...]).astype(BF)
    kpe = _rope(pa[:, MLA_Q_RANK + MLA_KV_RANK:], cm, s1m, s2m, hm)
    q = _dot(cqn, wq_ref[...])
    kn = _dot(ckvn, wk_ref[...])
    vv = _dot(ckvn, wv_ref[...])
    for h in range(MLA_HEADS):
        sl = slice(h * LANE, (h + 1) * LANE)
        qm_ref[0, h] = (_rope(q[:, sl], cm, s1m, s2m, hm) * MLA_SCALE).astype(BF)
        km_ref[0, h] = (kn[:, sl] + kpe).astype(BF)
        vm_ref[0, h] = (vv[:, sl] + one_col).astype(BF)

    pn = _dot(xb, wn_ref[...])
    low = lane < NSA_HD

    def halves(v):
        return jnp.where(low, v, 0.0), jnp.where(low, pltpu.roll(v, NSA_HD, 1), 0.0)

    def slab(i):
        return pn[:, i * LANE:(i + 1) * LANE]

    for i in range(NSA_HEADS // 2):
        for j, qh in enumerate(halves(_rope(slab(i), cn, s1n, s2n, hn) * NSA_SCALE)):
            nq_ref[0, 2 * i + j] = qh.astype(BF)
    o = NSA_HEADS // 2
    kc_ref[0] = _rope(slab(o), cn, s1n, s2n, hn)
    vc_ref[0] = slab(o + 1)
    k_s = halves(_rope(slab(o + 2), cn, s1n, s2n, hn))
    v_s = halves(slab(o + 3))
    k_w = halves(_rope(slab(o + 4), cn, s1n, s2n, hn))
    v_w = halves(slab(o + 5))
    gates = jax.nn.sigmoid(slab(o + 6))
    for g in range(NSA_KV_HEADS):
        ks_ref[0, g] = (k_s[g] + blk_onehot).astype(BF)
        vs_ref[0, g] = (v_s[g] + one_col).astype(BF)
        kw_ref[0, g] = k_w[g].astype(BF)
        vw_ref[0, g] = (v_w[g] + one_col).astype(BF)
        gn_ref[0, g] = gates if g == 0 else pltpu.roll(gates, LANE - g * NSA_HPG * 3, 1)


def _inproj_call(x, w, tabs):
    B, T, D = x.shape
    tm = min(TM_ROWS, T)
    H, G = MLA_HEADS, NSA_KV_HEADS
    head_spec = pl.BlockSpec((1, H, tm, LANE), lambda b, i: (b, 0, i, 0))
    grp_spec = pl.BlockSpec((1, G, tm, LANE), lambda b, i: (b, 0, i, 0))
    row_spec = pl.BlockSpec((1, tm, LANE), lambda b, i: (b, i, 0))
    tab_spec = pl.BlockSpec((tm, LANE), lambda b, i: (i, 0))
    head_shape = jax.ShapeDtypeStruct((B, H, T, LANE), BF)
    grp_shape = jax.ShapeDtypeStruct((B, G, T, LANE), BF)
    consts = [w["wa"], w["wn"], w["wq"], w["wk"], w["wv"], w["gq"], w["gkv"]]
    return pl.pallas_call(
        _inproj_kernel,
        grid=(B, T // tm),
        in_specs=[pl.BlockSpec((1, tm, D), lambda b, i: (b, i, 0))]
                 + [_const_spec(c.shape) for c in consts] + [tab_spec] * 6,
        out_specs=[head_spec] * 4 + [row_spec] * 2 + [grp_spec] * 4 + [grp_spec],
        out_shape=[head_shape] * 4 + [jax.ShapeDtypeStruct((B, T, LANE), F32)] * 2
                  + [grp_shape] * 4 + [jax.ShapeDtypeStruct((B, G, T, LANE), F32)],
        compiler_params=_params("parallel", "parallel"),
        name="in_proj",
    )(x, *consts, *tabs)


def _gelu_tanh(x):
    return 0.5 * x * (1.0 + jnp.tanh(np.sqrt(2.0 / np.pi) * (x + 0.044715 * (x * x * x))))


def _compress_kernel(kc_ref, vc_ref, pek_ref, pev_ref, w1k_ref, w1v_ref, b1k_ref, b1v_ref,
                     w2k_ref, w2vt_ref, kco_ref, vcto_ref):
    nc = kc_ref.shape[1]
    one_row = (lax.broadcasted_iota(jnp.int32, (LANE, nc), 0) == ONE_LANE).astype(F32)
    for src, pe, w1, b1, w2, dst, transposed in (
            (kc_ref, pek_ref, w1k_ref, b1k_ref, w2k_ref, kco_ref, False),
            (vc_ref, pev_ref, w1v_ref, b1v_ref, w2vt_ref, vcto_ref, True)):
        r = src[0]
        a_lo = (r + pe[0:1]).astype(BF)
        a_hi = (r + pe[1:2]).astype(BF)
        for g in range(NSA_KV_HEADS):
            h_hi = _dot(a_hi, w1[g, 1])
            h = _dot(a_lo, w1[g, 0]) + pltpu.roll(h_hi, nc - 1, 0) + b1[...]
            act = _gelu_tanh(h).astype(BF)
            if transposed:
                dst[0, g] = (_dot_nt(w2[...], act) + one_row).astype(BF)
            else:
                dst[0, g] = _dot(act, w2[...]).astype(BF)


def _compress_call(kc, vc, w):
    B, T, _ = kc.shape
    nc = T // S_CMP
    width = S_CMP * LANE
    kc2 = kc.reshape(B, nc, width)
    vc2 = vc.reshape(B, nc, width)
    consts = [w["pek"], w["pev"], w["w1k"], w["w1v"], w["b1k"], w["b1v"], w["w2k"], w["w2vt"]]
    in_spec = pl.BlockSpec((1, nc, width), lambda b: (b, 0, 0))
    return pl.pallas_call(
        _compress_kernel,
        grid=(B,),
        in_specs=[in_spec, in_spec] + [_const_spec(c.shape) for c in consts],
        out_specs=[pl.BlockSpec((1, NSA_KV_HEADS, nc, LANE), lambda b: (b, 0, 0, 0)),
                   pl.BlockSpec((1, NSA_KV_HEADS, LANE, nc), lambda b: (b, 0, 0, 0))],
        out_shape=[jax.ShapeDtypeStruct((B, NSA_KV_HEADS, nc, LANE), BF),
                   jax.ShapeDtypeStruct((B, NSA_KV_HEADS, LANE, nc), BF)],
        compiler_params=_params("parallel"),
        name="nsa_compress",
    )(kc2, vc2, *consts)


def _cmp_select_kernel(q_ref, kc_ref, vct_ref, ovt_ref, oc_ref, sel_ref, imp_ref, cnt_ref, *, n_sel):
    tq = q_ref.shape[2]
    nc = kc_ref.shape[2]
    ns = ovt_ref.shape[0]
    qs = pl.program_id(2) * tq
    kc = kc_ref[0, 0]
    vct = vct_ref[0, 0]

    t_c = lax.broadcasted_iota(jnp.int32, (nc, tq), 1) + qs
    c_c = lax.broadcasted_iota(jnp.int32, (nc, tq), 0)
    bias = jnp.where(c_c * S_CMP + (L_CMP - 1) <= t_c, 0.0, NEG)
    has_valid = jnp.where(lax.broadcasted_iota(jnp.int32, (1, tq), 1) + qs >= L_CMP - 1, 1.0, 0.0)
    bias = jnp.concatenate([bias] * NSA_HPG, axis=1)
    has_valid = jnp.concatenate([has_valid] * NSA_HPG, axis=1)
    st = _dot_nt(kc, q_ref[0].reshape(NSA_HPG * tq, LANE)) + bias
    et = jnp.exp2(st - st.max(0, keepdims=True))
    p = et * (has_valid / et.sum(0, keepdims=True))
    acc_t = _dot(vct, et.astype(BF))
    o_t = acc_t * (has_valid / acc_t[ONE_LANE:ONE_LANE + 1, :])
    p_sum = p[:, :tq]
    for h in range(NSA_HPG):
        if h:
            p_sum = p_sum + p[:, h * tq:(h + 1) * tq]
        oc_ref[0, h] = o_t[:, h * tq:(h + 1) * tq].T[:, :NSA_HD]

    hi = p_sum.astype(BF)
    lo = (p_sum - hi.astype(F32)).astype(BF)
    imp = _dot(ovt_ref[...], hi) + _dot(ovt_ref[...], lo)
    j = lax.broadcasted_iota(jnp.int32, (ns, tq), 0)
    cur = (lax.broadcasted_iota(jnp.int32, (ns, tq), 1) + qs) >> LOG2_L_SLC
    forced = (j == 0) | (j == cur) | (j == cur - 1)
    imp_ref[...] = jnp.where(forced, BIG, jnp.where(j <= cur, imp, NEG))
    cnt_ref[...] = jnp.zeros((ns, tq), F32)
    cur_max = (qs + tq - 1) >> LOG2_L_SLC
    n_grp = ns // SUBLANE
    j_in = lax.broadcasted_iota(jnp.int32, (SUBLANE, tq), 0)
    for kg in range(n_grp):
        @pl.when((kg * SUBLANE <= cur_max) & (cur_max >= n_sel))
        def _():
            grps = [imp_ref[r * SUBLANE:(r + 1) * SUBLANE, :] for r in range(n_grp)]
            cnts = [cnt_ref[r * SUBLANE:(r + 1) * SUBLANE, :] for r in range(n_grp)]
            for k in range(kg * SUBLANE, (kg + 1) * SUBLANE):
                row_k = imp_ref[k:k + 1, :]
                for r, grp in enumerate(grps):
                    if r > kg:
                        ahead = jnp.where(row_k >= grp, 1.0, 0.0)
                    elif r < kg:
                        ahead = jnp.where(row_k > grp, 1.0, 0.0)
                    else:
                        ahead = jnp.where(j_in > k % SUBLANE, jnp.where(row_k >= grp, 1.0, 0.0),
                                          jnp.where(row_k > grp, 1.0, 0.0))
                    cnts[r] = cnts[r] + ahead
            for r in range(n_grp):
                cnt_ref[r * SUBLANE:(r + 1) * SUBLANE, :] = cnts[r]
    not_sel = jnp.where(cnt_ref[...] < n_sel, 0.0, 1.0)
    pieces = [jnp.zeros((ONE_LANE, tq), F32), not_sel]
    if ns < LANE - ONE_LANE:
        pieces.append(jnp.zeros((LANE - ONE_LANE - ns, tq), F32))
    sel_ref[0, 0] = (jnp.concatenate(pieces, axis=0).T * NEG).astype(BF)


def _cmp_select_call(nq, kcc, vct, ovt, n_sel):
    B, H, T, _ = nq.shape
    G = NSA_KV_HEADS
    nc = kcc.shape[2]
    ns = ovt.shape[0]
    tq = min(TQ_CMP, T)
    return pl.pallas_call(
        functools.partial(_cmp_select_kernel, n_sel=n_sel),
        grid=(B, G, T // tq),
        in_specs=[pl.BlockSpec((1, NSA_HPG, tq, LANE), lambda b, g, i: (b, g, i, 0)),
                  pl.BlockSpec((1, 1, nc, LANE), lambda b, g, i: (b, g, 0, 0)),
                  pl.BlockSpec((1, 1, LANE, nc), lambda b, g, i: (b, g, 0, 0)),
                  _const_spec(ovt.shape)],
        out_specs=[pl.BlockSpec((1, NSA_HPG, tq, NSA_HD), lambda b, g, i: (b, g, i, 0)),
                   pl.BlockSpec((1, 1, tq, LANE), lambda b, g, i: (b, g, i, 0))],
        out_shape=[jax.ShapeDtypeStruct((B, H, T, NSA_HD), F32),
                   jax.ShapeDtypeStruct((B, G, T, LANE), BF)],
        scratch_shapes=[pltpu.VMEM((ns, tq), F32), pltpu.VMEM((ns, tq), F32)],
        compiler_params=_params("parallel", "parallel", "parallel"),
        name="nsa_cmp_select",
    )(nq, kcc, vct, ovt)


def _flash_step(q, k, v, m, acc, bias=None):
    s = _dot_nt(q, k)
    if bias is not None:
        reps = s.shape[0] // bias.shape[0]
        s = s + bias if reps == 1 else (s.reshape(reps, *bias.shape) + bias[None]).reshape(s.shape)
    m_new = jnp.maximum(m, s.max(-1, keepdims=True))
    p = jnp.exp2(s - m_new)
    acc = acc * jnp.exp2(m - m_new) + _dot(p.astype(BF), v)
    return m_new, acc


def _mla_kernel(q_ref, k_ref, v_ref, o_ref):
    nh, tq = q_ref.shape[1], q_ref.shape[2]
    qi = pl.program_id(2)
    causal_bias = jnp.where(lax.broadcasted_iota(jnp.int32, (tq, tq), 1)
                            <= lax.broadcasted_iota(jnp.int32, (tq, tq), 0), 0.0, NEG)

    def step(off, carry, bias):
        out = []
        for hh in range(nh):
            out += _flash_step(q_ref[0, hh], k_ref[0, hh, pl.ds(off, tq), :], v_ref[0, hh, pl.ds(off, tq), :],
                               carry[2 * hh], carry[2 * hh + 1], bias)
        return tuple(out)

    carry = (jnp.full((tq, 1), NEG, F32), jnp.zeros((tq, LANE), F32)) * nh
    carry = lax.fori_loop(0, qi, lambda j, c: step(pl.multiple_of(j * tq, tq), c, None), carry)
    carry = step(pl.multiple_of(qi * tq, tq), carry, causal_bias)
    outs = [carry[2 * hh + 1][:, :MLA_V] / carry[2 * hh + 1][:, ONE_LANE:ONE_LANE + 1] for hh in range(nh)]
    o_ref[0] = jnp.concatenate(outs, axis=1).astype(o_ref.dtype)


def _mla_call(qm, km, vm):
    B, H, T, _ = qm.shape
    tq = min(TQ_MLA, T)
    nh = MLA_HEADS_PER_STEP
    kv_spec = pl.BlockSpec((1, nh, T, LANE), lambda b, h, i: (b, h, 0, 0))
    return pl.pallas_call(
        _mla_kernel,
        grid=(B, H // nh, T // tq),
        in_specs=[pl.BlockSpec((1, nh, tq, LANE), lambda b, h, i: (b, h, i, 0)), kv_spec, kv_spec],
        out_specs=pl.BlockSpec((1, tq, nh * MLA_V), lambda b, h, i: (b, i, h)),
        out_shape=jax.ShapeDtypeStruct((B, T, H * MLA_V), BF),
        compiler_params=_params("parallel", "parallel", "parallel"),
        name="mla_attention",
    )(qm, km, vm)


def _nsa_kernel(q_ref, sel_ref, ks_ref, vs_ref, kw_ref, vw_ref, oc_ref, gn_ref, o_ref, *, tk):
    tq = q_ref.shape[2]
    rows = NSA_HPG * tq
    qs = pl.program_id(2) * tq
    qa = (q_ref[0] + sel_ref[0, 0][None]).reshape(rows, LANE)
    qpos = qs + lax.broadcasted_iota(jnp.int32, (tq, 1), 0)

    def body(j, carry):
        off = pl.multiple_of(j * tk, tk)
        return _flash_step(qa, ks_ref[0, 0, pl.ds(off, tk), :], vs_ref[0, 0, pl.ds(off, tk), :], *carry)

    jd = qs // tk
    carry = (jnp.full((rows, 1), NEG, F32), jnp.zeros((rows, LANE), F32))
    m, acc = lax.fori_loop(0, jd, body, carry)
    off = pl.multiple_of(jd * tk, tk)
    kpos = off + lax.broadcasted_iota(jnp.int32, (tq, tk), 1)
    m, acc = _flash_step(qa, ks_ref[0, 0, pl.ds(off, tk), :], vs_ref[0, 0, pl.ds(off, tk), :],
                         m, acc, jnp.where(kpos <= qpos, 0.0, NEG))
    o_s = acc[:, :NSA_HD] / acc[:, ONE_LANE:ONE_LANE + 1]

    span = WINDOW + tq
    ws = pl.multiple_of(jnp.maximum(qs - WINDOW, 0), tq)
    kpos_w = ws + lax.broadcasted_iota(jnp.int32, (tq, span), 1)
    bias_w = jnp.where(kpos_w <= qpos, jnp.where(kpos_w > qpos - WINDOW, 0.0, NEG), NEG)
    _, acc_w = _flash_step(qa, kw_ref[0, 0, pl.ds(ws, span), :], vw_ref[0, 0, pl.ds(ws, span), :],
                           jnp.full((rows, 1), NEG, F32), jnp.zeros((rows, LANE), F32), bias_w)
    o_w = acc_w[:, :NSA_HD] / acc_w[:, ONE_LANE:ONE_LANE + 1]

    gn = gn_ref[0, 0]
    outs = []
    for h in range(NSA_HPG):
        r = slice(h * tq, (h + 1) * tq)
        outs.append(gn[:, 3 * h:3 * h + 1] * oc_ref[0, h]
                    + gn[:, 3 * h + 1:3 * h + 2] * o_s[r]
                    + gn[:, 3 * h + 2:3 * h + 3] * o_w[r])
    o_ref[0] = jnp.concatenate(outs, axis=1).astype(o_ref.dtype)


def _nsa_call(nq, sel, ks, vs, kw, vw, oc, gn):
    B, H, T, _ = nq.shape
    G = NSA_KV_HEADS
    tq = min(TQ_NSA, T)
    tk = min(TK_NSA, T)
    assert tk % tq == 0 and WINDOW % tq == 0 and T >= WINDOW + tq
    kv_spec = pl.BlockSpec((1, 1, T, LANE), lambda b, g, i: (b, g, 0, 0))
    grp_spec = pl.BlockSpec((1, 1, tq, LANE), lambda b, g, i: (b, g, i, 0))
    return pl.pallas_call(
        functools.partial(_nsa_kernel, tk=tk),
        grid=(B, G, T // tq),
        in_specs=[pl.BlockSpec((1, NSA_HPG, tq, LANE), lambda b, g, i: (b, g, i, 0)),
                  grp_spec, kv_spec, kv_spec, kv_spec, kv_spec,
                  pl.BlockSpec((1, NSA_HPG, tq, NSA_HD), lambda b, g, i: (b, g, i, 0)),
                  grp_spec],
        out_specs=pl.BlockSpec((1, tq, NSA_HPG * NSA_HD), lambda b, g, i: (b, i, g)),
        out_shape=jax.ShapeDtypeStruct((B, T, H * NSA_HD), BF),
        compiler_params=_params("parallel", "parallel", "parallel"),
        name="nsa_select_window",
    )(nq, sel, ks, vs, kw, vw, oc, gn)


def _merge_kernel(x_ref, om_ref, on_ref, wgm_ref, wgn_ref, wpm_ref, wpn_ref, wo_ref, g_ref, b_ref, o_ref):
    x = x_ref[...]
    xb = x.astype(BF)
    y = (jax.nn.sigmoid(_dot(xb, wgm_ref[...])) * _dot(om_ref[...], wpm_ref[...])
         + jax.nn.sigmoid(_dot(xb, wgn_ref[...])) * _dot(on_ref[...], wpn_ref[...]))
    mix = _dot(y.astype(BF), wo_ref[...])
    o_ref[...] = _ln(ALPHA * x + mix, g_ref[...], b_ref[...])


def _merge_call(x2d, om, on, w):
    n, d = x2d.shape
    tm = min(TM_ROWS, n)
    consts = [w["wgm"], w["wgn"], w["wpm"], w["wpn"], w["wo"], w["ln_mix_g"], w["ln_mix_b"]]
    return pl.pallas_call(
        _merge_kernel,
        grid=(n // tm,),
        in_specs=[pl.BlockSpec((tm, d), lambda i: (i, 0)),
                  pl.BlockSpec((tm, om.shape[1]), lambda i: (i, 0)),
                  pl.BlockSpec((tm, on.shape[1]), lambda i: (i, 0))]
                 + [_const_spec(c.shape) for c in consts],
        out_specs=pl.BlockSpec((tm, d), lambda i: (i, 0)),
        out_shape=jax.ShapeDtypeStruct((n, d), F32),
        compiler_params=_params("parallel"),
        name="merge_out_ln",
    )(x2d, om, on, *consts)


def _pad_cols(w, width):
    return jnp.pad(w, ((0, 0), (0, width - w.shape[1])))


def _layer_weights(l, p):
    w_in = p["w_in"][l]
    d = w_in.shape[0]
    z64 = jnp.zeros((d, 64), F32)
    w = {}
    w["wa"] = jnp.concatenate(
        [w_in[:, OFF_CQ:OFF_KPE], z64, w_in[:, OFF_KPE:OFF_NQ], jnp.zeros((d, 32), F32)], axis=1).astype(BF)
    w["wn"] = jnp.concatenate([w_in[:, OFF_NQ:OFF_GN], _pad_cols(w_in[:, OFF_GN:OFF_GM], LANE)], axis=1).astype(BF)

    w_uq = p["w_uq"][l].reshape(MLA_Q_RANK, MLA_HEADS, MLA_NOPE + MLA_ROPE)
    w["wq"] = jnp.pad(w_uq, ((0, 0), (0, 0), (0, LANE - MLA_NOPE - MLA_ROPE))).reshape(MLA_Q_RANK, -1).astype(BF)
    w_ukv = p["w_ukv"][l].reshape(MLA_KV_RANK, MLA_HEADS, MLA_NOPE + MLA_V)
    w["wk"] = jnp.pad(w_ukv[:, :, :MLA_NOPE], ((0, 0), (0, 0), (0, LANE - MLA_NOPE))).reshape(MLA_KV_RANK, -1).astype(BF)
    w["wv"] = jnp.pad(w_ukv[:, :, MLA_NOPE:], ((0, 0), (0, 0), (0, LANE - MLA_V))).reshape(MLA_KV_RANK, -1).astype(BF)
    w["gq"] = p["q_norm_g"][l][None]
    w["gkv"] = p["kv_norm_g"][l][None]

    for nm, pe, w1, b1, w2 in (("k", "cmp_pe_k", "cmp_k_w1", "cmp_k_b1", "cmp_k_w2"),
                               ("v", "cmp_pe_v", "cmp_v_w1", "cmp_v_b1", "cmp_v_w2")):
        pe_l = p[pe][l].reshape(2, S_CMP, 1, NSA_HD)
        w["pe" + nm] = jnp.broadcast_to(pe_l, (2, S_CMP, NSA_KV_HEADS, NSA_HD)).reshape(2, S_CMP * LANE)
        w1_l = p[w1][l].reshape(2, S_CMP, 1, NSA_HD, CMP_HIDDEN)
        per_g = []
        for g in range(NSA_KV_HEADS):
            mask = (jnp.arange(NSA_KV_HEADS) == g).astype(F32).reshape(1, 1, NSA_KV_HEADS, 1, 1)
            per_g.append((w1_l * mask).reshape(2, S_CMP * LANE, CMP_HIDDEN))
        w["w1" + nm] = jnp.stack(per_g).astype(BF)
        w["b1" + nm] = p[b1][l][None]
        w2_l = _pad_cols(p[w2][l], LANE).astype(BF)
        w["w2" + nm] = w2_l
        w["w2" + nm + "t"] = w2_l.T

    w["wgm"] = w_in[:, OFF_GM:OFF_GNS].astype(BF)
    w["wgn"] = w_in[:, OFF_GNS:].astype(BF)
    w["wpm"] = p["w_proj_mla"][l].astype(BF)
    w["wpn"] = p["w_proj_nsa"][l].astype(BF)
    w["wo"] = p["w_out"][l].astype(BF)
    w["ln_mix_g"] = p["ln_mix_g"][l][None]
    w["ln_mix_b"] = p["ln_mix_b"][l][None]
    for nm in ("ffn1", "ffn2"):
        for s in ("wg", "wu", "wd"):
            w[nm + s] = p[nm + "_" + s][l].astype(BF)
    for nm in ("ln_f1", "ln_f2"):
        w[nm + "_g"] = p[nm + "_g"][l][None]
        w[nm + "_b"] = p[nm + "_b"][l][None]
    return w


def _rope_tables(T):
    def tables(rot_dim):
        inv = 1.0 / (ROPE_THETA ** (jnp.arange(0, rot_dim, 2, dtype=F32) / rot_dim))
        ang = jnp.arange(T, dtype=F32)[:, None] * inv[None, :]
        return jnp.cos(ang), jnp.sin(ang)

    cos_m, sin_m = tables(MLA_ROPE)
    cos_n, sin_n = tables(NSA_ROT)
    one = lambda n: jnp.ones((T, n), F32)
    zero = lambda n: jnp.zeros((T, n), F32)
    hm, hn = MLA_ROPE // 2, NSA_ROT // 2
    rest_m = LANE - MLA_NOPE - MLA_ROPE
    cm = jnp.concatenate([one(MLA_NOPE), cos_m, cos_m, one(rest_m)], 1)
    s1m = jnp.concatenate([zero(MLA_NOPE), -sin_m, zero(hm), zero(rest_m)], 1)
    s2m = jnp.concatenate([zero(MLA_NOPE), zero(hm), sin_m, zero(rest_m)], 1)
    rest_n = NSA_HD - NSA_ROT
    cn = jnp.tile(jnp.concatenate([cos_n, cos_n, one(rest_n)], 1), (1, 2))
    s1n = jnp.tile(jnp.concatenate([-sin_n, zero(hn), zero(rest_n)], 1), (1, 2))
    s2n = jnp.tile(jnp.concatenate([zero(hn), sin_n, zero(rest_n)], 1), (1, 2))
    return cm, s1m, s2m, cn, s1n, s2n


def _overlap_t(T):
    nc, ns = T // S_CMP, T // L_SLC
    n_cmp = (T - L_CMP) // S_CMP + 1
    start = np.arange(nc) * S_CMP
    blk = np.arange(ns)
    ov = ((start[None, :] < (blk[:, None] + 1) * L_SLC) & (start[None, :] + L_CMP > blk[:, None] * L_SLC)
          & (np.arange(nc)[None, :] < n_cmp))
    return jnp.asarray(ov.astype(np.float32), dtype=BF)


def kernel(x, ln_f1_g, ln_f1_b, ffn1_wg, ffn1_wu, ffn1_wd, w_in, q_norm_g, w_uq, kv_norm_g, w_ukv,
           cmp_pe_k, cmp_k_w1, cmp_k_b1, cmp_k_w2, cmp_pe_v, cmp_v_w1, cmp_v_b1, cmp_v_w2,
           w_proj_mla, w_proj_nsa, w_out, ln_mix_g, ln_mix_b,
           ffn2_wg, ffn2_wu, ffn2_wd, ln_f2_g, ln_f2_b):
    p = dict(ln_f1_g=ln_f1_g, ln_f1_b=ln_f1_b, ffn1_wg=ffn1_wg, ffn1_wu=ffn1_wu, ffn1_wd=ffn1_wd,
             w_in=w_in, q_norm_g=q_norm_g, w_uq=w_uq, kv_norm_g=kv_norm_g, w_ukv=w_ukv,
             cmp_pe_k=cmp_pe_k, cmp_k_w1=cmp_k_w1, cmp_k_b1=cmp_k_b1, cmp_k_w2=cmp_k_w2,
             cmp_pe_v=cmp_pe_v, cmp_v_w1=cmp_v_w1, cmp_v_b1=cmp_v_b1, cmp_v_w2=cmp_v_w2,
             w_proj_mla=w_proj_mla, w_proj_nsa=w_proj_nsa, w_out=w_out,
             ln_mix_g=ln_mix_g, ln_mix_b=ln_mix_b, ffn2_wg=ffn2_wg, ffn2_wu=ffn2_wu, ffn2_wd=ffn2_wd,
             ln_f2_g=ln_f2_g, ln_f2_b=ln_f2_b)
    B, T, D = x.shape
    assert T % L_SLC == 0 and T // L_SLC <= LANE - ONE_LANE
    tabs = _rope_tables(T)
    ovt = _overlap_t(T)
    n_sel = min(N_SEL, T // L_SLC)
    h = x.reshape(B * T, D)
    for l in range(DEPTH):
        w = _layer_weights(l, p)
        h = _ffn_call(h, w["ffn1wg"], w["ffn1wu"], w["ffn1wd"], w["ln_f1_g"], w["ln_f1_b"])
        qm, km, vm, nq, kc, vc, ks, vs, kw, vw, gn = _inproj_call(h.reshape(B, T, D), w, tabs)
        kcc, vct = _compress_call(kc, vc, w)
        oc, sel = _cmp_select_call(nq, kcc, vct, ovt, n_sel)
        om = _mla_call(qm, km, vm)
        on = _nsa_call(nq, sel, ks, vs, kw, vw, oc, gn)
        h = _merge_call(h, om.reshape(B * T, -1), on.reshape(B * T, -1), w)
        h = _ffn_call(h, w["ffn2wg"], w["ffn2wu"], w["ffn2wd"], w["ln_f2_g"], w["ln_f2_b"])
    return h.reshape(B, T, D)
```

```python
import functools

import numpy as np
import jax
import jax.numpy as jnp
from jax import lax
from jax.experimental import pallas as pl
from jax.experimental.pallas import tpu as pltpu

D_MODEL = 1024
DEPTH = 2
MLA_HEADS = 8
MLA_NOPE = 64
MLA_ROPE = 32
MLA_V = 64
MLA_Q_RANK = 256
MLA_KV_RANK = 128
NSA_HEADS = 8
NSA_KV_HEADS = 2
NSA_HPG = NSA_HEADS // NSA_KV_HEADS
NSA_HD = 64
NSA_ROT = NSA_HD // 4
L_CMP = 32
S_CMP = 16
CMP_HIDDEN = 128
L_SLC = 64
LOG2_L_SLC = 6
N_SEL = 16
WINDOW = 512
ROPE_THETA = 500000.0
D_FF = 2816
ALPHA = (2 * DEPTH) ** 0.25
LN_EPS = 1e-5
RMS_EPS = 1e-6
NEG = -1e30
BIG = 1e30
LOG2E = 1.4426950408889634
MLA_SCALE = (MLA_NOPE + MLA_ROPE) ** -0.5 * LOG2E
NSA_SCALE = NSA_HD ** -0.5 * LOG2E

OFF_CQ = 0
OFF_CKV = OFF_CQ + MLA_Q_RANK
OFF_KPE = OFF_CKV + MLA_KV_RANK
OFF_NQ = OFF_KPE + MLA_ROPE
OFF_NKV = OFF_NQ + NSA_HEADS * NSA_HD
OFF_GN = OFF_NKV + 6 * NSA_KV_HEADS * NSA_HD
OFF_GM = OFF_GN + 3 * NSA_HEADS
OFF_GNS = OFF_GM + D_MODEL

LANE = 128
SUBLANE = 8
ONE_LANE = 64
VMEM_LIMIT = 56 * 1024 * 1024

BF = jnp.bfloat16
F32 = jnp.float32

MXU_DIM = 256
FF_CHUNKS = ((0, 6 * MXU_DIM), (6 * MXU_DIM, D_FF))
TM_ROWS = 512
TQ_MLA = 1024
MLA_HEADS_PER_STEP = 2
TQ_CMP = 256
TQ_NSA = 256
TK_NSA = 512


def _dot(a, b):
    return jnp.dot(a, b, preferred_element_type=F32)


def _dot_nt(a, b):
    return lax.dot_general(a, b, (((1,), (1,)), ((), ())), preferred_element_type=F32)


def _ln(y, g, b):
    mu = jnp.mean(y, -1, keepdims=True)
    d = y - mu
    var = jnp.mean(d * d, -1, keepdims=True)
    return d * lax.rsqrt(var + LN_EPS) * g + b


def _params(*sem):
    return pltpu.CompilerParams(dimension_semantics=sem, vmem_limit_bytes=VMEM_LIMIT)


def _const_spec(shape):
    n = len(shape)
    return pl.BlockSpec(shape, lambda *_: (0,) * n)


def _ffn_ln(x, wg_ref, wu_ref, wd_ref, g, b):
    xb = x.astype(BF)
    acc = jnp.zeros(x.shape, F32)
    for c0, c1 in FF_CHUNKS:
        hg = _dot(xb, wg_ref[:, c0:c1])
        hu = _dot(xb, wu_ref[:, c0:c1])
        h = (hg * jax.nn.sigmoid(hg)) * hu
        acc = acc + _dot(h.astype(BF), wd_ref[c0:c1, :])
    return _ln(ALPHA * x + 0.5 * acc, g, b)


def _ffn_kernel(x_ref, wg_ref, wu_ref, wd_ref, g_ref, b_ref, o_ref):
    o_ref[...] = _ffn_ln(x_ref[...], wg_ref, wu_ref, wd_ref, g_ref[...], b_ref[...])


def _ffn_call(x2d, wg, wu, wd, g, b):
    n, d = x2d.shape
    tm = min(TM_ROWS, n)
    return pl.pallas_call(
        _ffn_kernel,
        grid=(n // tm,),
        in_specs=[pl.BlockSpec((tm, d), lambda i: (i, 0)),
                  _const_spec(wg.shape), _const_spec(wu.shape), _const_spec(wd.shape),
                  _const_spec(g.shape), _const_spec(b.shape)],
        out_specs=pl.BlockSpec((tm, d), lambda i: (i, 0)),
        out_shape=jax.ShapeDtypeStruct((n, d), F32),
        compiler_params=_params("parallel"),
        name="ffn_ln",
    )(x2d, wg, wu, wd, g, b)


def _rope(v, c, s1, s2, half):
    return v * c + pltpu.roll(v, LANE - half, 1) * s1 + pltpu.roll(v, half, 1) * s2


def _rms(v, g):
    return v * lax.rsqrt(jnp.mean(v * v, -1, keepdims=True) + RMS_EPS) * g


def _inproj_kernel(x_ref, wa_ref, wn_ref, wq_ref, wqr_ref, wk_ref, wv_ref, gq_ref, gkv_ref,
                   cm_ref, s1m_ref, s2m_ref, cn_ref, s1n_ref, s2n_ref,
                   qm_ref, km_ref, vm_ref, nq_ref, kc_ref, vc_ref,
                   ks_ref, vs_ref, kw_ref, vw_ref, gn_ref):
    tm = x_ref.shape[1]
    t0 = pl.program_id(1) * tm
    xb = x_ref[0].astype(BF)
    lane = lax.broadcasted_iota(jnp.int32, (tm, LANE), 1)
    row = lax.broadcasted_iota(jnp.int32, (tm, LANE), 0) + t0
    one_col = (lane == ONE_LANE).astype(F32)
    blk_onehot = (lane - ONE_LANE == (row >> LOG2_L_SLC)).astype(F32)
    cm, s1m, s2m = cm_ref[...], s1m_ref[...], s2m_ref[...]
    cn, s1n, s2n = cn_ref[...], s1n_ref[...], s2n_ref[...]
    hm = MLA_ROPE // 2
    hn = NSA_ROT // 2

    pa = _dot(xb, wa_ref[...])
    pn = _dot(xb, wn_ref[...])
    cqn = _rms(pa[:, :MLA_Q_RANK], gq_ref[...]).astype(BF)
    ckvn = _rms(pa[:, MLA_Q_RANK:MLA_Q_RANK + MLA_KV_RANK], gkv_ref[...]).astype(BF)
    kpe = _rope(pa[:, MLA_Q_RANK + MLA_KV_RANK:], cm, s1m, s2m, hm)
    q = _dot(cqn, wq_ref[...])
    q_rot = _dot(cqn, wqr_ref[...])
    kn = _dot(ckvn, wk_ref[...])
    vv = _dot(ckvn, wv_ref[...])
    c_q, s_q = cm * MLA_SCALE, (s2m - s1m) * MLA_SCALE
    for h in range(MLA_HEADS):
        sl = slice(h * LANE, (h + 1) * LANE)
        qm_ref[0, h] = (q[:, sl] * c_q + q_rot[:, sl] * s_q).astype(BF)
        km_ref[0, h] = (kn[:, sl] + kpe).astype(BF)
        vm_ref[0, h] = (vv[:, sl] + one_col).astype(BF)

    low = lane < NSA_HD

    def halves(v):
        return jnp.where(low, v, 0.0), jnp.where(low, pltpu.roll(v, NSA_HD, 1), 0.0)

    def slab(i):
        return pn[:, i * LANE:(i + 1) * LANE]

    for i in range(NSA_HEADS // 2):
        for j, qh in enumerate(halves(_rope(slab(i), cn, s1n, s2n, hn) * NSA_SCALE)):
            nq_ref[0, 2 * i + j] = qh.astype(BF)
    o = NSA_HEADS // 2
    kc_ref[0] = _rope(slab(o), cn, s1n, s2n, hn)
    vc_ref[0] = slab(o + 1)
    k_s = halves(_rope(slab(o + 2), cn, s1n, s2n, hn))
    v_s = halves(slab(o + 3))
    k_w = halves(_rope(slab(o + 4), cn, s1n, s2n, hn))
    v_w = halves(slab(o + 5))
    gates = jax.nn.sigmoid(slab(o + 6))
    for g in range(NSA_KV_HEADS):
        ks_ref[0, g] = (k_s[g] + blk_onehot).astype(BF)
        vs_ref[0, g] = (v_s[g] + one_col).astype(BF)
        kw_ref[0, g] = k_w[g].astype(BF)
        vw_ref[0, g] = (v_w[g] + one_col).astype(BF)
        gn_ref[0, g] = gates if g == 0 else pltpu.roll(gates, LANE - g * NSA_HPG * 3, 1)


def _inproj_call(x, w, tabs):
    B, T, D = x.shape
    tm = min(TM_ROWS, T)
    H, G = MLA_HEADS, NSA_KV_HEADS
    head_spec = pl.BlockSpec((1, H, tm, LANE), lambda b, i: (b, 0, i, 0))
    grp_spec = pl.BlockSpec((1, G, tm, LANE), lambda b, i: (b, 0, i, 0))
    row_spec = pl.BlockSpec((1, tm, LANE), lambda b, i: (b, i, 0))
    tab_spec = pl.BlockSpec((tm, LANE), lambda b, i: (i, 0))
    head_shape = jax.ShapeDtypeStruct((B, H, T, LANE), BF)
    grp_shape = jax.ShapeDtypeStruct((B, G, T, LANE), BF)
    consts = [w["wa"], w["wn"], w["wq"], w["wqr"], w["wk"], w["wv"], w["gq"], w["gkv"]]
    return pl.pallas_call(
        _inproj_kernel,
        grid=(B, T // tm),
        in_specs=[pl.BlockSpec((1, tm, D), lambda b, i: (b, i, 0))]
                 + [_const_spec(c.shape) for c in consts] + [tab_spec] * 6,
        out_specs=[head_spec] * 4 + [row_spec] * 2 + [grp_spec] * 4 + [grp_spec],
        out_shape=[head_shape] * 4 + [jax.ShapeDtypeStruct((B, T, LANE), F32)] * 2
                  + [grp_shape] * 4 + [jax.ShapeDtypeStruct((B, G, T, LANE), F32)],
        compiler_params=_params("parallel", "parallel"),
        name="in_proj",
    )(x, *consts, *tabs)


def _gelu_tanh(x):
    return 0.5 * x * (1.0 + jnp.tanh(np.sqrt(2.0 / np.pi) * (x + 0.044715 * (x * x * x))))


def _compress_kernel(kc_ref, vc_ref, pek_ref, pev_ref, w1k_ref, w1v_ref, b1k_ref, b1v_ref,
                     w2k_ref, w2vt_ref, kco_ref, vcto_ref):
    nc = kc_ref.shape[1]
    one_row = (lax.broadcasted_iota(jnp.int32, (LANE, nc), 0) == ONE_LANE).astype(F32)
    for src, pe, w1, b1, w2, dst, transposed in (
            (kc_ref, pek_ref, w1k_ref, b1k_ref, w2k_ref, kco_ref, False),
            (vc_ref, pev_ref, w1v_ref, b1v_ref, w2vt_ref, vcto_ref, True)):
        r = src[0]
        a_lo = (r + pe[0:1]).astype(BF)
        a_hi = (r + pe[1:2]).astype(BF)
        for g in range(NSA_KV_HEADS):
            h_hi = _dot(a_hi, w1[g, 1])
            h = _dot(a_lo, w1[g, 0]) + pltpu.roll(h_hi, nc - 1, 0) + b1[...]
            act = _gelu_tanh(h).astype(BF)
            if transposed:
                dst[0, g] = (_dot_nt(w2[...], act) + one_row).astype(BF)
            else:
                dst[0, g] = _dot(act, w2[...]).astype(BF)


def _compress_call(kc, vc, w):
    B, T, _ = kc.shape
    nc = T // S_CMP
    width = S_CMP * LANE
    kc2 = kc.reshape(B, nc, width)
    vc2 = vc.reshape(B, nc, width)
    consts = [w["pek"], w["pev"], w["w1k"], w["w1v"], w["b1k"], w["b1v"], w["w2k"], w["w2vt"]]
    in_spec = pl.BlockSpec((1, nc, width), lambda b: (b, 0, 0))
    return pl.pallas_call(
        _compress_kernel,
        grid=(B,),
        in_specs=[in_spec, in_spec] + [_const_spec(c.shape) for c in consts],
        out_specs=[pl.BlockSpec((1, NSA_KV_HEADS, nc, LANE), lambda b: (b, 0, 0, 0)),
                   pl.BlockSpec((1, NSA_KV_HEADS, LANE, nc), lambda b: (b, 0, 0, 0))],
        out_shape=[jax.ShapeDtypeStruct((B, NSA_KV_HEADS, nc, LANE), BF),
                   jax.ShapeDtypeStruct((B, NSA_KV_HEADS, LANE, nc), BF)],
        compiler_params=_params("parallel"),
        name="nsa_compress",
    )(kc2, vc2, *consts)


def _cmp_select_kernel(q_ref, kc_ref, vct_ref, ovt_ref, oc_ref, sel_ref, imp_ref, cnt_ref, *, n_sel):
    tq = q_ref.shape[2]
    nc = kc_ref.shape[2]
    ns = ovt_ref.shape[0]
    qs = pl.program_id(2) * tq
    kc = kc_ref[0, 0]
    vct = vct_ref[0, 0]

    t_c = lax.broadcasted_iota(jnp.int32, (nc, tq), 1) + qs
    c_c = lax.broadcasted_iota(jnp.int32, (nc, tq), 0)
    bias = jnp.where(c_c * S_CMP + (L_CMP - 1) <= t_c, 0.0, NEG)
    has_valid = jnp.where(lax.broadcasted_iota(jnp.int32, (1, tq), 1) + qs >= L_CMP - 1, 1.0, 0.0)
    bias = jnp.concatenate([bias] * NSA_HPG, axis=1)
    has_valid = jnp.concatenate([has_valid] * NSA_HPG, axis=1)
    st = _dot_nt(kc, q_ref[0].reshape(NSA_HPG * tq, LANE)) + bias
    et = jnp.exp2(st - st.max(0, keepdims=True))
    p = et * (has_valid / et.sum(0, keepdims=True))
    acc_t = _dot(vct, et.astype(BF))
    o_t = acc_t * (has_valid / acc_t[ONE_LANE:ONE_LANE + 1, :])
    p_sum = p[:, :tq]
    for h in range(NSA_HPG):
        if h:
            p_sum = p_sum + p[:, h * tq:(h + 1) * tq]
        oc_ref[0, h] = o_t[:, h * tq:(h + 1) * tq].T[:, :NSA_HD]

    hi = p_sum.astype(BF)
    lo = (p_sum - hi.astype(F32)).astype(BF)
    imp = _dot(ovt_ref[...], hi) + _dot(ovt_ref[...], lo)
    j = lax.broadcasted_iota(jnp.int32, (ns, tq), 0)
    cur = (lax.broadcasted_iota(jnp.int32, (ns, tq), 1) + qs) >> LOG2_L_SLC
    forced = (j == 0) | (j == cur) | (j == cur - 1)
    imp_ref[...] = jnp.where(forced, BIG, jnp.where(j <= cur, imp, NEG))
    cnt_ref[...] = jnp.zeros((ns, tq), F32)
    cur_max = (qs + tq - 1) >> LOG2_L_SLC
    n_grp = ns // SUBLANE
    j_in = lax.broadcasted_iota(jnp.int32, (SUBLANE, tq), 0)
    for kg in range(n_grp):
        @pl.when((kg * SUBLANE <= cur_max) & (cur_max >= n_sel))
        def _():
            grps = [imp_ref[r * SUBLANE:(r + 1) * SUBLANE, :] for r in range(n_grp)]
            cnts = [cnt_ref[r * SUBLANE:(r + 1) * SUBLANE, :] for r in range(n_grp)]
            for k in range(kg * SUBLANE, (kg + 1) * SUBLANE):
                row_k = imp_ref[k:k + 1, :]
                for r, grp in enumerate(grps):
                    if r > kg:
                        ahead = jnp.where(row_k >= grp, 1.0, 0.0)
                    elif r < kg:
                        ahead = jnp.where(row_k > grp, 1.0, 0.0)
                    else:
                        ahead = jnp.where(j_in > k % SUBLANE, jnp.where(row_k >= grp, 1.0, 0.0),
                                          jnp.where(row_k > grp, 1.0, 0.0))
                    cnts[r] = cnts[r] + ahead
            for r in range(n_grp):
                cnt_ref[r * SUBLANE:(r + 1) * SUBLANE, :] = cnts[r]
    not_sel = jnp.where(cnt_ref[...] < n_sel, 0.0, 1.0)
    pieces = [jnp.zeros((ONE_LANE, tq), F32), not_sel]
    if ns < LANE - ONE_LANE:
        pieces.append(jnp.zeros((LANE - ONE_LANE - ns, tq), F32))
    sel_ref[0, 0] = (jnp.concatenate(pieces, axis=0).T * NEG).astype(BF)


def _cmp_select_call(nq, kcc, vct, ovt, n_sel):
    B, H, T, _ = nq.shape
    G = NSA_KV_HEADS
    nc = kcc.shape[2]
    ns = ovt.shape[0]
    tq = min(TQ_CMP, T)
    return pl.pallas_call(
        functools.partial(_cmp_select_kernel, n_sel=n_sel),
        grid=(B, G, T // tq),
        in_specs=[pl.BlockSpec((1, NSA_HPG, tq, LANE), lambda b, g, i: (b, g, i, 0)),
                  pl.BlockSpec((1, 1, nc, LANE), lambda b, g, i: (b, g, 0, 0)),
                  pl.BlockSpec((1, 1, LANE, nc), lambda b, g, i: (b, g, 0, 0)),
                  _const_spec(ovt.shape)],
        out_specs=[pl.BlockSpec((1, NSA_HPG, tq, NSA_HD), lambda b, g, i: (b, g, i, 0)),
                   pl.BlockSpec((1, 1, tq, LANE), lambda b, g, i: (b, g, i, 0))],
        out_shape=[jax.ShapeDtypeStruct((B, H, T, NSA_HD), F32),
                   jax.ShapeDtypeStruct((B, G, T, LANE), BF)],
        scratch_shapes=[pltpu.VMEM((ns, tq), F32), pltpu.VMEM((ns, tq), F32)],
        compiler_params=_params("parallel", "parallel", "parallel"),
        name="nsa_cmp_select",
    )(nq, kcc, vct, ovt)


def _flash_step(q, k, v, m, acc, bias=None):
    s = _dot_nt(q, k)
    if bias is not None:
        reps = s.shape[0] // bias.shape[0]
        s = s + bias if reps == 1 else (s.reshape(reps, *bias.shape) + bias[None]).reshape(s.shape)
    m_new = jnp.maximum(m, s.max(-1, keepdims=True))
    p = jnp.exp2(s - m_new)
    acc = acc * jnp.exp2(m - m_new) + _dot(p.astype(BF), v)
    return m_new, acc


def _mla_kernel(q_ref, k_ref, v_ref, o_ref):
    nh, tq = q_ref.shape[1], q_ref.shape[2]
    qi = pl.program_id(2)
    half = tq // 2
    causal_bias = jnp.where(lax.broadcasted_iota(jnp.int32, (half, half), 1)
                            <= lax.broadcasted_iota(jnp.int32, (half, half), 0), 0.0, NEG)
    lower_bias = jnp.concatenate([jnp.zeros((half, half), F32), causal_bias], axis=1)

    def step(off, carry, bias):
        out = []
        for hh in range(nh):
            out += _flash_step(q_ref[0, hh], k_ref[0, hh, pl.ds(off, tq), :], v_ref[0, hh, pl.ds(off, tq), :],
                               carry[2 * hh], carry[2 * hh + 1], bias)
        return tuple(out)

    carry = (jnp.full((tq, 1), NEG, F32), jnp.zeros((tq, LANE), F32)) * nh
    carry = lax.fori_loop(0, qi, lambda j, c: step(pl.multiple_of(j * tq, tq), c, None), carry)
    off = pl.multiple_of(qi * tq, tq)
    outs = []
    for hh in range(nh):
        m, acc = carry[2 * hh], carry[2 * hh + 1]
        _, acc_t = _flash_step(q_ref[0, hh, :half, :], k_ref[0, hh, pl.ds(off, half), :],
                               v_ref[0, hh, pl.ds(off, half), :], m[:half], acc[:half], causal_bias)
        _, acc_b = _flash_step(q_ref[0, hh, half:, :], k_ref[0, hh, pl.ds(off, tq), :],
                               v_ref[0, hh, pl.ds(off, tq), :], m[half:], acc[half:], lower_bias)
        acc = jnp.concatenate([acc_t, acc_b], axis=0)
        outs.append(acc[:, :MLA_V] / acc[:, ONE_LANE:ONE_LANE + 1])
    o_ref[0] = jnp.concatenate(outs, axis=1).astype(o_ref.dtype)


def _mla_call(qm, km, vm):
    B, H, T, _ = qm.shape
    tq = min(TQ_MLA, T)
    nh = MLA_HEADS_PER_STEP
    kv_spec = pl.BlockSpec((1, nh, T, LANE), lambda b, h, i: (b, h, 0, 0))
    return pl.pallas_call(
        _mla_kernel,
        grid=(B, H // nh, T // tq),
        in_specs=[pl.BlockSpec((1, nh, tq, LANE), lambda b, h, i: (b, h, i, 0)), kv_spec, kv_spec],
        out_specs=pl.BlockSpec((1, tq, nh * MLA_V), lambda b, h, i: (b, i, h)),
        out_shape=jax.ShapeDtypeStruct((B, T, H * MLA_V), BF),
        compiler_params=_params("parallel", "parallel", "parallel"),
        name="mla_attention",
    )(qm, km, vm)


def _nsa_kernel(q_ref, sel_ref, ks_ref, vs_ref, kw_ref, vw_ref, oc_ref, gn_ref, o_ref, *, tk):
    tq = q_ref.shape[2]
    rows = NSA_HPG * tq
    qs = pl.program_id(2) * tq
    qa = (q_ref[0] + sel_ref[0, 0][None]).reshape(rows, LANE)
    qpos = qs + lax.broadcasted_iota(jnp.int32, (tq, 1), 0)

    def body(j, carry):
        off = pl.multiple_of(j * tk, tk)
        return _flash_step(qa, ks_ref[0, 0, pl.ds(off, tk), :], vs_ref[0, 0, pl.ds(off, tk), :], *carry)

    jd = qs // tk
    carry = (jnp.full((rows, 1), NEG, F32), jnp.zeros((rows, LANE), F32))
    m, acc = lax.fori_loop(0, jd, body, carry)
    off = pl.multiple_of(jd * tk, tk)
    kpos = off + lax.broadcasted_iota(jnp.int32, (tq, tk), 1)
    m, acc = _flash_step(qa, ks_ref[0, 0, pl.ds(off, tk), :], vs_ref[0, 0, pl.ds(off, tk), :],
                         m, acc, jnp.where(kpos <= qpos, 0.0, NEG))
    o_s = acc[:, :NSA_HD] / acc[:, ONE_LANE:ONE_LANE + 1]

    span = WINDOW + tq
    ws = pl.multiple_of(jnp.maximum(qs - WINDOW, 0), tq)
    kpos_w = ws + lax.broadcasted_iota(jnp.int32, (tq, span), 1)
    bias_w = jnp.where(kpos_w <= qpos, jnp.where(kpos_w > qpos - WINDOW, 0.0, NEG), NEG)
    _, acc_w = _flash_step(qa, kw_ref[0, 0, pl.ds(ws, span), :], vw_ref[0, 0, pl.ds(ws, span), :],
                           jnp.full((rows, 1), NEG, F32), jnp.zeros((rows, LANE), F32), bias_w)
    o_w = acc_w[:, :NSA_HD] / acc_w[:, ONE_LANE:ONE_LANE + 1]

    gn = gn_ref[0, 0]
    outs = []
    for h in range(NSA_HPG):
        r = slice(h * tq, (h + 1) * tq)
        outs.append(gn[:, 3 * h:3 * h + 1] * oc_ref[0, h]
                    + gn[:, 3 * h + 1:3 * h + 2] * o_s[r]
                    + gn[:, 3 * h + 2:3 * h + 3] * o_w[r])
    o_ref[0] = jnp.concatenate(outs, axis=1).astype(o_ref.dtype)


def _nsa_call(nq, sel, ks, vs, kw, vw, oc, gn):
    B, H, T, _ = nq.shape
    G = NSA_KV_HEADS
    tq = min(TQ_NSA, T)
    tk = min(TK_NSA, T)
    assert tk % tq == 0 and WINDOW % tq == 0 and T >= WINDOW + tq
    kv_spec = pl.BlockSpec((1, 1, T, LANE), lambda b, g, i: (b, g, 0, 0))
    grp_spec = pl.BlockSpec((1, 1, tq, LANE), lambda b, g, i: (b, g, i, 0))
    return pl.pallas_call(
        functools.partial(_nsa_kernel, tk=tk),
        grid=(B, G, T // tq),
        in_specs=[pl.BlockSpec((1, NSA_HPG, tq, LANE), lambda b, g, i: (b, g, i, 0)),
                  grp_spec, kv_spec, kv_spec, kv_spec, kv_spec,
                  pl.BlockSpec((1, NSA_HPG, tq, NSA_HD), lambda b, g, i: (b, g, i, 0)),
                  grp_spec],
        out_specs=pl.BlockSpec((1, tq, NSA_HPG * NSA_HD), lambda b, g, i: (b, i, g)),
        out_shape=jax.ShapeDtypeStruct((B, T, H * NSA_HD), BF),
        compiler_params=_params("parallel", "parallel", "parallel"),
        name="nsa_select_window",
    )(nq, sel, ks, vs, kw, vw, oc, gn)


def _merge_kernel(x_ref, om_ref, on_ref, wgm_ref, wgn_ref, wpm_ref, wpn_ref, wo_ref, g_ref, b_ref, o_ref):
    x = x_ref[...]
    xb = x.astype(BF)
    y = (jax.nn.sigmoid(_dot(xb, wgm_ref[...])) * _dot(om_ref[...], wpm_ref[...])
         + jax.nn.sigmoid(_dot(xb, wgn_ref[...])) * _dot(on_ref[...], wpn_ref[...]))
    mix = _dot(y.astype(BF), wo_ref[...])
    o_ref[...] = _ln(ALPHA * x + mix, g_ref[...], b_ref[...])


def _merge_call(x2d, om, on, w):
    n, d = x2d.shape
    tm = min(TM_ROWS, n)
    consts = [w["wgm"], w["wgn"], w["wpm"], w["wpn"], w["wo"], w["ln_mix_g"], w["ln_mix_b"]]
    return pl.pallas_call(
        _merge_kernel,
        grid=(n // tm,),
        in_specs=[pl.BlockSpec((tm, d), lambda i: (i, 0)),
                  pl.BlockSpec((tm, om.shape[1]), lambda i: (i, 0)),
                  pl.BlockSpec((tm, on.shape[1]), lambda i: (i, 0))]
                 + [_const_spec(c.shape) for c in consts],
        out_specs=pl.BlockSpec((tm, d), lambda i: (i, 0)),
        out_shape=jax.ShapeDtypeStruct((n, d), F32),
        compiler_params=_params("parallel"),
        name="merge_out_ln",
    )(x2d, om, on, *consts)


def _pad_cols(w, width):
    return jnp.pad(w, ((0, 0), (0, width - w.shape[1])))


def _layer_weights(l, p):
    w_in = p["w_in"][l]
    d = w_in.shape[0]
    z64 = jnp.zeros((d, 64), F32)
    w = {}
    w["wa"] = jnp.concatenate(
        [w_in[:, OFF_CQ:OFF_KPE], z64, w_in[:, OFF_KPE:OFF_NQ], jnp.zeros((d, 32), F32)], axis=1).astype(BF)
    w["wn"] = jnp.concatenate([w_in[:, OFF_NQ:OFF_GN], _pad_cols(w_in[:, OFF_GN:OFF_GM], LANE)], axis=1).astype(BF)

    w_uq = p["w_uq"][l].reshape(MLA_Q_RANK, MLA_HEADS, MLA_NOPE + MLA_ROPE)
    w["wq"] = jnp.pad(w_uq, ((0, 0), (0, 0), (0, LANE - MLA_NOPE - MLA_ROPE))).reshape(MLA_Q_RANK, -1).astype(BF)
    pe1, pe2 = w_uq[:, :, MLA_NOPE:MLA_NOPE + MLA_ROPE // 2], w_uq[:, :, MLA_NOPE + MLA_ROPE // 2:]
    w["wqr"] = jnp.pad(jnp.concatenate([-pe2, pe1], axis=2),
                       ((0, 0), (0, 0), (MLA_NOPE, LANE - MLA_NOPE - MLA_ROPE))).reshape(MLA_Q_RANK, -1).astype(BF)
    w_ukv = p["w_ukv"][l].reshape(MLA_KV_RANK, MLA_HEADS, MLA_NOPE + MLA_V)
    w["wk"] = jnp.pad(w_ukv[:, :, :MLA_NOPE], ((0, 0), (0, 0), (0, LANE - MLA_NOPE))).reshape(MLA_KV_RANK, -1).astype(BF)
    w["wv"] = jnp.pad(w_ukv[:, :, MLA_NOPE:], ((0, 0), (0, 0), (0, LANE - MLA_V))).reshape(MLA_KV_RANK, -1).astype(BF)
    w["gq"] = p["q_norm_g"][l][None]
    w["gkv"] = p["kv_norm_g"][l][None]

    for nm, pe, w1, b1, w2 in (("k", "cmp_pe_k", "cmp_k_w1", "cmp_k_b1", "cmp_k_w2"),
                               ("v", "cmp_pe_v", "cmp_v_w1", "cmp_v_b1", "cmp_v_w2")):
        pe_l = p[pe][l].reshape(2, S_CMP, 1, NSA_HD)
        w["pe" + nm] = jnp.broadcast_to(pe_l, (2, S_CMP, NSA_KV_HEADS, NSA_HD)).reshape(2, S_CMP * LANE)
        w1_l = p[w1][l].reshape(2, S_CMP, 1, NSA_HD, CMP_HIDDEN)
        per_g = []
        for g in range(NSA_KV_HEADS):
            mask = (jnp.arange(NSA_KV_HEADS) == g).astype(F32).reshape(1, 1, NSA_KV_HEADS, 1, 1)
            per_g.append((w1_l * mask).reshape(2, S_CMP * LANE, CMP_HIDDEN))
        w["w1" + nm] = jnp.stack(per_g).astype(BF)
        w["b1" + nm] = p[b1][l][None]
        w2_l = _pad_cols(p[w2][l], LANE).astype(BF)
        w["w2" + nm] = w2_l
        w["w2" + nm + "t"] = w2_l.T

    w["wgm"] = w_in[:, OFF_GM:OFF_GNS].astype(BF)
    w["wgn"] = w_in[:, OFF_GNS:].astype(BF)
    w["wpm"] = p["w_proj_mla"][l].astype(BF)
    w["wpn"] = p["w_proj_nsa"][l].astype(BF)
    w["wo"] = p["w_out"][l].astype(BF)
    w["ln_mix_g"] = p["ln_mix_g"][l][None]
    w["ln_mix_b"] = p["ln_mix_b"][l][None]
    for nm in ("ffn1", "ffn2"):
        for s in ("wg", "wu", "wd"):
            w[nm + s] = p[nm + "_" + s][l].astype(BF)
    for nm in ("ln_f1", "ln_f2"):
        w[nm + "_g"] = p[nm + "_g"][l][None]
        w[nm + "_b"] = p[nm + "_b"][l][None]
    return w


def _rope_tables(T):
    def tables(rot_dim):
        inv = 1.0 / (ROPE_THETA ** (jnp.arange(0, rot_dim, 2, dtype=F32) / rot_dim))
        ang = jnp.arange(T, dtype=F32)[:, None] * inv[None, :]
        return jnp.cos(ang), jnp.sin(ang)

    cos_m, sin_m = tables(MLA_ROPE)
    cos_n, sin_n = tables(NSA_ROT)
    one = lambda n: jnp.ones((T, n), F32)
    zero = lambda n: jnp.zeros((T, n), F32)
    hm, hn = MLA_ROPE // 2, NSA_ROT // 2
    rest_m = LANE - MLA_NOPE - MLA_ROPE
    cm = jnp.concatenate([one(MLA_NOPE), cos_m, cos_m, one(rest_m)], 1)
    s1m = jnp.concatenate([zero(MLA_NOPE), -sin_m, zero(hm), zero(rest_m)], 1)
    s2m = jnp.concatenate([zero(MLA_NOPE), zero(hm), sin_m, zero(rest_m)], 1)
    rest_n = NSA_HD - NSA_ROT
    cn = jnp.tile(jnp.concatenate([cos_n, cos_n, one(rest_n)], 1), (1, 2))
    s1n = jnp.tile(jnp.concatenate([-sin_n, zero(hn), zero(rest_n)], 1), (1, 2))
    s2n = jnp.tile(jnp.concatenate([zero(hn), sin_n, zero(rest_n)], 1), (1, 2))
    return cm, s1m, s2m, cn, s1n, s2n


def _overlap_t(T):
    nc, ns = T // S_CMP, T // L_SLC
    n_cmp = (T - L_CMP) // S_CMP + 1
    start = np.arange(nc) * S_CMP
    blk = np.arange(ns)
    ov = ((start[None, :] < (blk[:, None] + 1) * L_SLC) & (start[None, :] + L_CMP > blk[:, None] * L_SLC)
          & (np.arange(nc)[None, :] < n_cmp))
    return jnp.asarray(ov.astype(np.float32), dtype=BF)


def kernel(x, ln_f1_g, ln_f1_b, ffn1_wg, ffn1_wu, ffn1_wd, w_in, q_norm_g, w_uq, kv_norm_g, w_ukv,
           cmp_pe_k, cmp_k_w1, cmp_k_b1, cmp_k_w2, cmp_pe_v, cmp_v_w1, cmp_v_b1, cmp_v_w2,
           w_proj_mla, w_proj_nsa, w_out, ln_mix_g, ln_mix_b,
           ffn2_wg, ffn2_wu, ffn2_wd, ln_f2_g, ln_f2_b):
    p = dict(ln_f1_g=ln_f1_g, ln_f1_b=ln_f1_b, ffn1_wg=ffn1_wg, ffn1_wu=ffn1_wu, ffn1_wd=ffn1_wd,
             w_in=w_in, q_norm_g=q_norm_g, w_uq=w_uq, kv_norm_g=kv_norm_g, w_ukv=w_ukv,
             cmp_pe_k=cmp_pe_k, cmp_k_w1=cmp_k_w1, cmp_k_b1=cmp_k_b1, cmp_k_w2=cmp_k_w2,
             cmp_pe_v=cmp_pe_v, cmp_v_w1=cmp_v_w1, cmp_v_b1=cmp_v_b1, cmp_v_w2=cmp_v_w2,
             w_proj_mla=w_proj_mla, w_proj_nsa=w_proj_nsa, w_out=w_out,
             ln_mix_g=ln_mix_g, ln_mix_b=ln_mix_b, ffn2_wg=ffn2_wg, ffn2_wu=ffn2_wu, ffn2_wd=ffn2_wd,
             ln_f2_g=ln_f2_g, ln_f2_b=ln_f2_b)
    B, T, D = x.shape
    assert T % L_SLC == 0 and T // L_SLC <= LANE - ONE_LANE
    tabs = _rope_tables(T)
    ovt = _overlap_t(T)
    n_sel = min(N_SEL, T // L_SLC)
    h = x.reshape(B * T, D)
    for l in range(DEPTH):
        w = _layer_weights(l, p)
        h = _ffn_call(h, w["ffn1wg"], w["ffn1wu"], w["ffn1wd"], w["ln_f1_g"], w["ln_f1_b"])
        qm, km, vm, nq, kc, vc, ks, vs, kw, vw, gn = _inproj_call(h.reshape(B, T, D), w, tabs)
        kcc, vct = _compress_call(kc, vc, w)
        oc, sel = _cmp_select_call(nq, kcc, vct, ovt, n_sel)
        om = _mla_call(qm, km, vm)
        on = _nsa_call(nq, sel, ks, vs, kw, vw, oc, gn)
        h = _merge_call(h, om.reshape(B * T, -1), on.reshape(B * T, -1), w)
        h = _ffn_call(h, w["ffn2wg"], w["ffn2wu"], w["ffn2wd"], w["ln_f2_g"], w["ln_f2_b"])
    return h.reshape(B, T, D)
```

```python
import functools

import numpy as np
import jax
import jax.numpy as jnp
from jax import lax
from jax.experimental import pallas as pl
from jax.experimental.pallas import tpu as pltpu

D_MODEL = 1024
DEPTH = 2
MLA_HEADS = 8
MLA_NOPE = 64
MLA_ROPE = 32
MLA_V = 64
MLA_Q_RANK = 256
MLA_KV_RANK = 128
NSA_HEADS = 8
NSA_KV_HEADS = 2
NSA_HPG = NSA_HEADS // NSA_KV_HEADS
NSA_HD = 64
NSA_ROT = NSA_HD // 4
L_CMP = 32
S_CMP = 16
CMP_HIDDEN = 128
L_SLC = 64
LOG2_L_SLC = 6
N_SEL = 16
WINDOW = 512
ROPE_THETA = 500000.0
D_FF = 2816
ALPHA = (2 * DEPTH) ** 0.25
LN_EPS = 1e-5
RMS_EPS = 1e-6
NEG = -1e30
BIG = 1e30
LOG2E = 1.4426950408889634
MLA_SCALE = (MLA_NOPE + MLA_ROPE) ** -0.5 * LOG2E
NSA_SCALE = NSA_HD ** -0.5 * LOG2E

OFF_CQ = 0
OFF_CKV = OFF_CQ + MLA_Q_RANK
OFF_KPE = OFF_CKV + MLA_KV_RANK
OFF_NQ = OFF_KPE + MLA_ROPE
OFF_NKV = OFF_NQ + NSA_HEADS * NSA_HD
OFF_GN = OFF_NKV + 6 * NSA_KV_HEADS * NSA_HD
OFF_GM = OFF_GN + 3 * NSA_HEADS
OFF_GNS = OFF_GM + D_MODEL

LANE = 128
SUBLANE = 8
ONE_LANE = 64
VMEM_LIMIT = 56 * 1024 * 1024

BF = jnp.bfloat16
F32 = jnp.float32

MXU_DIM = 256
FF_CHUNKS = ((0, 6 * MXU_DIM), (6 * MXU_DIM, D_FF))
TM_ROWS = 512
FFN_SUBTILES = 2
TQ_MLA = 1024
MLA_HEADS_PER_STEP = 2
TQ_CMP = 256
TQ_NSA = 256
TK_NSA = 512


def _dot(a, b):
    return jnp.dot(a, b, preferred_element_type=F32)


def _dot_nt(a, b):
    return lax.dot_general(a, b, (((1,), (1,)), ((), ())), preferred_element_type=F32)


def _ln(y, g, b):
    mu = jnp.mean(y, -1, keepdims=True)
    d = y - mu
    var = jnp.mean(d * d, -1, keepdims=True)
    return d * lax.rsqrt(var + LN_EPS) * g + b


def _params(*sem):
    return pltpu.CompilerParams(dimension_semantics=sem, vmem_limit_bytes=VMEM_LIMIT)


def _const_spec(shape):
    n = len(shape)
    return pl.BlockSpec(shape, lambda *_: (0,) * n, pipeline_mode=pl.Buffered(1))


def _layer_spec(stacked, l):
    n = stacked.ndim - 1
    return pl.BlockSpec((None,) + stacked.shape[1:], lambda *_: (l,) + (0,) * n, pipeline_mode=pl.Buffered(1))


def _ffn_ln(x, wg_ref, wu_ref, wd_ref, g, b):
    xb = x.astype(BF)
    acc = jnp.zeros(x.shape, F32)
    for c0, c1 in FF_CHUNKS:
        hg = _dot(xb, wg_ref[:, c0:c1])
        hu = _dot(xb, wu_ref[:, c0:c1])
        h = (hg * jax.nn.sigmoid(hg)) * hu
        acc = acc + _dot(h.astype(BF), wd_ref[c0:c1, :])
    return _ln(ALPHA * x + 0.5 * acc, g, b)


def _ffn_kernel(x_ref, wg_ref, wu_ref, wd_ref, g_ref, b_ref, o_ref):
    sub = x_ref.shape[0] // FFN_SUBTILES
    for r in range(FFN_SUBTILES):
        rows = slice(r * sub, (r + 1) * sub)
        o_ref[rows, :] = _ffn_ln(x_ref[rows, :], wg_ref, wu_ref, wd_ref, g_ref[...], b_ref[...])


def _ffn_call(x2d, l, wg, wu, wd, g, b):
    n, d = x2d.shape
    tm = min(TM_ROWS * FFN_SUBTILES, n)
    return pl.pallas_call(
        _ffn_kernel,
        grid=(n // tm,),
        in_specs=[pl.BlockSpec((tm, d), lambda i: (i, 0)),
                  _layer_spec(wg, l), _layer_spec(wu, l), _layer_spec(wd, l), _layer_spec(g, l), _layer_spec(b, l)],
        out_specs=pl.BlockSpec((tm, d), lambda i: (i, 0)),
        out_shape=jax.ShapeDtypeStruct((n, d), F32),
        compiler_params=_params("parallel"),
        name="ffn_ln",
    )(x2d, wg, wu, wd, g, b)


def _rope(v, c, s1, s2, half):
    return v * c + pltpu.roll(v, LANE - half, 1) * s1 + pltpu.roll(v, half, 1) * s2


def _rms(v, g):
    return v * lax.rsqrt(jnp.mean(v * v, -1, keepdims=True) + RMS_EPS) * g


def _inproj_kernel(x_ref, wa_ref, wn_ref, wq_ref, wqr_ref, wk_ref, wv_ref, gq_ref, gkv_ref,
                   cm_ref, s1m_ref, s2m_ref, cn_ref, s1n_ref, s2n_ref,
                   qm_ref, km_ref, vm_ref, nq_ref, kc_ref, vc_ref,
                   ks_ref, vs_ref, kw_ref, vw_ref, gn_ref):
    tm = x_ref.shape[1]
    t0 = pl.program_id(1) * tm
    xb = x_ref[0].astype(BF)
    lane = lax.broadcasted_iota(jnp.int32, (tm, LANE), 1)
    row = lax.broadcasted_iota(jnp.int32, (tm, LANE), 0) + t0
    one_col = (lane == ONE_LANE).astype(F32)
    blk_onehot = (lane - ONE_LANE == (row >> LOG2_L_SLC)).astype(F32)
    cm, s1m, s2m = cm_ref[...], s1m_ref[...], s2m_ref[...]
    cn, s1n, s2n = cn_ref[...], s1n_ref[...], s2n_ref[...]
    hm = MLA_ROPE // 2
    hn = NSA_ROT // 2

    pa = _dot(xb, wa_ref[...])
    pn = _dot(xb, wn_ref[...])
    cqn = _rms(pa[:, :MLA_Q_RANK], gq_ref[...]).astype(BF)
    ckvn = _rms(pa[:, MLA_Q_RANK:MLA_Q_RANK + MLA_KV_RANK], gkv_ref[...]).astype(BF)
    kpe = _rope(pa[:, MLA_Q_RANK + MLA_KV_RANK:], cm, s1m, s2m, hm)
    q = _dot(cqn, wq_ref[...])
    q_rot = _dot(cqn, wqr_ref[...])
    kn = _dot(ckvn, wk_ref[...])
    vv = _dot(ckvn, wv_ref[...])
    c_q, s_q = cm * MLA_SCALE, (s2m - s1m) * MLA_SCALE
    for h in range(MLA_HEADS):
        sl = slice(h * LANE, (h + 1) * LANE)
        qm_ref[0, h] = (q[:, sl] * c_q + q_rot[:, sl] * s_q).astype(BF)
        km_ref[0, h] = (kn[:, sl] + kpe).astype(BF)
        vm_ref[0, h] = (vv[:, sl] + one_col).astype(BF)

    low = lane < NSA_HD

    def halves(v):
        return jnp.where(low, v, 0.0), jnp.where(low, pltpu.roll(v, NSA_HD, 1), 0.0)

    def slab(i):
        return pn[:, i * LANE:(i + 1) * LANE]

    for i in range(NSA_HEADS // 2):
        for j, qh in enumerate(halves(_rope(slab(i), cn, s1n, s2n, hn) * NSA_SCALE)):
            nq_ref[0, 2 * i + j] = qh.astype(BF)
    o = NSA_HEADS // 2
    kc_ref[0] = _rope(slab(o), cn, s1n, s2n, hn)
    vc_ref[0] = slab(o + 1)
    k_s = halves(_rope(slab(o + 2), cn, s1n, s2n, hn))
    v_s = halves(slab(o + 3))
    k_w = halves(_rope(slab(o + 4), cn, s1n, s2n, hn))
    v_w = halves(slab(o + 5))
    gates = jax.nn.sigmoid(slab(o + 6))
    for g in range(NSA_KV_HEADS):
        ks_ref[0, g] = (k_s[g] + blk_onehot).astype(BF)
        vs_ref[0, g] = (v_s[g] + one_col).astype(BF)
        kw_ref[0, g] = k_w[g].astype(BF)
        vw_ref[0, g] = (v_w[g] + one_col).astype(BF)
        gn_ref[0, g] = gates if g == 0 else pltpu.roll(gates, LANE - g * NSA_HPG * 3, 1)


def _inproj_call(x, w, l, tabs):
    B, T, D = x.shape
    tm = min(TM_ROWS, T)
    H, G = MLA_HEADS, NSA_KV_HEADS
    head_spec = pl.BlockSpec((1, H, tm, LANE), lambda b, i: (b, 0, i, 0))
    grp_spec = pl.BlockSpec((1, G, tm, LANE), lambda b, i: (b, 0, i, 0))
    row_spec = pl.BlockSpec((1, tm, LANE), lambda b, i: (b, i, 0))
    tab_spec = pl.BlockSpec((tm, LANE), lambda b, i: (i, 0))
    head_shape = jax.ShapeDtypeStruct((B, H, T, LANE), BF)
    grp_shape = jax.ShapeDtypeStruct((B, G, T, LANE), BF)
    consts = [w["wa"], w["wn"], w["wq"], w["wqr"], w["wk"], w["wv"], w["gq"], w["gkv"]]
    return pl.pallas_call(
        _inproj_kernel,
        grid=(B, T // tm),
        in_specs=[pl.BlockSpec((1, tm, D), lambda b, i: (b, i, 0))]
                 + [_layer_spec(c, l) for c in consts] + [tab_spec] * 6,
        out_specs=[head_spec] * 4 + [row_spec] * 2 + [grp_spec] * 4 + [grp_spec],
        out_shape=[head_shape] * 4 + [jax.ShapeDtypeStruct((B, T, LANE), F32)] * 2
                  + [grp_shape] * 4 + [jax.ShapeDtypeStruct((B, G, T, LANE), F32)],
        compiler_params=_params("parallel", "parallel"),
        name="in_proj",
    )(x, *consts, *tabs)


def _gelu_tanh(x):
    return 0.5 * x * (1.0 + jnp.tanh(np.sqrt(2.0 / np.pi) * (x + 0.044715 * (x * x * x))))


def _compress_kernel(kc_ref, vc_ref, pek_ref, pev_ref, w1k_ref, w1v_ref, b1k_ref, b1v_ref,
                     w2k_ref, w2vt_ref, kco_ref, vcto_ref):
    nc = kc_ref.shape[1]
    one_row = (lax.broadcasted_iota(jnp.int32, (LANE, nc), 0) == ONE_LANE).astype(F32)
    for src, pe, w1, b1, w2, dst, transposed in (
            (kc_ref, pek_ref, w1k_ref, b1k_ref, w2k_ref, kco_ref, False),
            (vc_ref, pev_ref, w1v_ref, b1v_ref, w2vt_ref, vcto_ref, True)):
        r = src[0]
        a_lo = (r + pe[0:1]).astype(BF)
        a_hi = (r + pe[1:2]).astype(BF)
        for g in range(NSA_KV_HEADS):
            h_hi = _dot(a_hi, w1[g, 1])
            h = _dot(a_lo, w1[g, 0]) + pltpu.roll(h_hi, nc - 1, 0) + b1[...]
            act = _gelu_tanh(h).astype(BF)
            if transposed:
                dst[0, g] = (_dot_nt(w2[...], act) + one_row).astype(BF)
            else:
                dst[0, g] = _dot(act, w2[...]).astype(BF)


def _compress_call(kc, vc, w, l):
    B, T, _ = kc.shape
    nc = T // S_CMP
    width = S_CMP * LANE
    kc2 = kc.reshape(B, nc, width)
    vc2 = vc.reshape(B, nc, width)
    consts = [w["pek"], w["pev"], w["w1k"], w["w1v"], w["b1k"], w["b1v"], w["w2k"], w["w2vt"]]
    in_spec = pl.BlockSpec((1, nc, width), lambda b: (b, 0, 0))
    return pl.pallas_call(
        _compress_kernel,
        grid=(B,),
        in_specs=[in_spec, in_spec] + [_layer_spec(c, l) for c in consts],
        out_specs=[pl.BlockSpec((1, NSA_KV_HEADS, nc, LANE), lambda b: (b, 0, 0, 0)),
                   pl.BlockSpec((1, NSA_KV_HEADS, LANE, nc), lambda b: (b, 0, 0, 0))],
        out_shape=[jax.ShapeDtypeStruct((B, NSA_KV_HEADS, nc, LANE), BF),
                   jax.ShapeDtypeStruct((B, NSA_KV_HEADS, LANE, nc), BF)],
        compiler_params=_params("parallel"),
        name="nsa_compress",
    )(kc2, vc2, *consts)


def _cmp_select_kernel(q_ref, kc_ref, vct_ref, ovt_ref, oc_ref, sel_ref, imp_ref, cnt_ref, *, n_sel):
    tq = q_ref.shape[2]
    nc = kc_ref.shape[2]
    ns = ovt_ref.shape[0]
    qs = pl.program_id(2) * tq
    kc = kc_ref[0, 0]
    vct = vct_ref[0, 0]

    t_c = lax.broadcasted_iota(jnp.int32, (nc, tq), 1) + qs
    c_c = lax.broadcasted_iota(jnp.int32, (nc, tq), 0)
    bias = jnp.where(c_c * S_CMP + (L_CMP - 1) <= t_c, 0.0, NEG)
    has_valid = jnp.where(lax.broadcasted_iota(jnp.int32, (1, tq), 1) + qs >= L_CMP - 1, 1.0, 0.0)
    bias = jnp.concatenate([bias] * NSA_HPG, axis=1)
    has_valid = jnp.concatenate([has_valid] * NSA_HPG, axis=1)
    st = _dot_nt(kc, q_ref[0].reshape(NSA_HPG * tq, LANE)) + bias
    et = jnp.exp2(st - st.max(0, keepdims=True))
    p = et * (has_valid / et.sum(0, keepdims=True))
    acc_t = _dot(vct, et.astype(BF))
    o_t = acc_t * (has_valid / acc_t[ONE_LANE:ONE_LANE + 1, :])
    p_sum = p[:, :tq]
    for h in range(NSA_HPG):
        if h:
            p_sum = p_sum + p[:, h * tq:(h + 1) * tq]
        oc_ref[0, h] = o_t[:, h * tq:(h + 1) * tq].T[:, :NSA_HD]

    hi = p_sum.astype(BF)
    lo = (p_sum - hi.astype(F32)).astype(BF)
    imp = _dot(ovt_ref[...], hi) + _dot(ovt_ref[...], lo)
    j = lax.broadcasted_iota(jnp.int32, (ns, tq), 0)
    cur = (lax.broadcasted_iota(jnp.int32, (ns, tq), 1) + qs) >> LOG2_L_SLC
    forced = (j == 0) | (j == cur) | (j == cur - 1)
    imp_ref[...] = jnp.where(forced, BIG, jnp.where(j <= cur, imp, NEG))
    cnt_ref[...] = jnp.zeros((ns, tq), F32)
    cur_max = (qs + tq - 1) >> LOG2_L_SLC
    n_grp = ns // SUBLANE
    j_in = lax.broadcasted_iota(jnp.int32, (SUBLANE, tq), 0)
    for kg in range(n_grp):
        @pl.when((kg * SUBLANE <= cur_max) & (cur_max >= n_sel))
        def _():
            grps = [imp_ref[r * SUBLANE:(r + 1) * SUBLANE, :] for r in range(n_grp)]
            cnts = [cnt_ref[r * SUBLANE:(r + 1) * SUBLANE, :] for r in range(n_grp)]
            for k in range(kg * SUBLANE, (kg + 1) * SUBLANE):
                row_k = imp_ref[k:k + 1, :]
                for r, grp in enumerate(grps):
                    if r > kg:
                        ahead = jnp.where(row_k >= grp, 1.0, 0.0)
                    elif r < kg:
                        ahead = jnp.where(row_k > grp, 1.0, 0.0)
                    else:
                        ahead = jnp.where(j_in > k % SUBLANE, jnp.where(row_k >= grp, 1.0, 0.0),
                                          jnp.where(row_k > grp, 1.0, 0.0))
                    cnts[r] = cnts[r] + ahead
            for r in range(n_grp):
                cnt_ref[r * SUBLANE:(r + 1) * SUBLANE, :] = cnts[r]
    not_sel = jnp.where(cnt_ref[...] < n_sel, 0.0, 1.0)
    pieces = [jnp.zeros((ONE_LANE, tq), F32), not_sel]
    if ns < LANE - ONE_LANE:
        pieces.append(jnp.zeros((LANE - ONE_LANE - ns, tq), F32))
    sel_ref[0, 0] = (jnp.concatenate(pieces, axis=0).T * NEG).astype(BF)


def _cmp_select_call(nq, kcc, vct, ovt, n_sel):
    B, H, T, _ = nq.shape
    G = NSA_KV_HEADS
    nc = kcc.shape[2]
    ns = ovt.shape[0]
    tq = min(TQ_CMP, T)
    return pl.pallas_call(
        functools.partial(_cmp_select_kernel, n_sel=n_sel),
        grid=(B, G, T // tq),
        in_specs=[pl.BlockSpec((1, NSA_HPG, tq, LANE), lambda b, g, i: (b, g, i, 0)),
                  pl.BlockSpec((1, 1, nc, LANE), lambda b, g, i: (b, g, 0, 0)),
                  pl.BlockSpec((1, 1, LANE, nc), lambda b, g, i: (b, g, 0, 0)),
                  _const_spec(ovt.shape)],
        out_specs=[pl.BlockSpec((1, NSA_HPG, tq, NSA_HD), lambda b, g, i: (b, g, i, 0)),
                   pl.BlockSpec((1, 1, tq, LANE), lambda b, g, i: (b, g, i, 0))],
        out_shape=[jax.ShapeDtypeStruct((B, H, T, NSA_HD), F32),
                   jax.ShapeDtypeStruct((B, G, T, LANE), BF)],
        scratch_shapes=[pltpu.VMEM((ns, tq), F32), pltpu.VMEM((ns, tq), F32)],
        compiler_params=_params("parallel", "parallel", "parallel"),
        name="nsa_cmp_select",
    )(nq, kcc, vct, ovt)


def _flash_step(q, k, v, m, acc, bias=None):
    s = _dot_nt(q, k)
    if bias is not None:
        reps = s.shape[0] // bias.shape[0]
        s = s + bias if reps == 1 else (s.reshape(reps, *bias.shape) + bias[None]).reshape(s.shape)
    m_new = jnp.maximum(m, s.max(-1, keepdims=True))
    p = jnp.exp2(s - m_new)
    acc = acc * jnp.exp2(m - m_new) + _dot(p.astype(BF), v)
    return m_new, acc


def _mla_kernel(q_ref, k_ref, v_ref, o_ref):
    nh, tq = q_ref.shape[1], q_ref.shape[2]
    qi = pl.program_id(2)
    half = tq // 2
    causal_bias = jnp.where(lax.broadcasted_iota(jnp.int32, (half, half), 1)
                            <= lax.broadcasted_iota(jnp.int32, (half, half), 0), 0.0, NEG)
    lower_bias = jnp.concatenate([jnp.zeros((half, half), F32), causal_bias], axis=1)

    def step(off, carry, bias):
        out = []
        for hh in range(nh):
            out += _flash_step(q_ref[0, hh], k_ref[0, hh, pl.ds(off, tq), :], v_ref[0, hh, pl.ds(off, tq), :],
                               carry[2 * hh], carry[2 * hh + 1], bias)
        return tuple(out)

    carry = (jnp.full((tq, 1), NEG, F32), jnp.zeros((tq, LANE), F32)) * nh
    carry = lax.fori_loop(0, qi, lambda j, c: step(pl.multiple_of(j * tq, tq), c, None), carry)
    off = pl.multiple_of(qi * tq, tq)
    outs = []
    for hh in range(nh):
        m, acc = carry[2 * hh], carry[2 * hh + 1]
        _, acc_t = _flash_step(q_ref[0, hh, :half, :], k_ref[0, hh, pl.ds(off, half), :],
                               v_ref[0, hh, pl.ds(off, half), :], m[:half], acc[:half], causal_bias)
        _, acc_b = _flash_step(q_ref[0, hh, half:, :], k_ref[0, hh, pl.ds(off, tq), :],
                               v_ref[0, hh, pl.ds(off, tq), :], m[half:], acc[half:], lower_bias)
        acc = jnp.concatenate([acc_t, acc_b], axis=0)
        outs.append(acc[:, :MLA_V] / acc[:, ONE_LANE:ONE_LANE + 1])
    o_ref[0] = jnp.concatenate(outs, axis=1).astype(o_ref.dtype)


def _mla_call(qm, km, vm):
    B, H, T, _ = qm.shape
    tq = min(TQ_MLA, T)
    nh = MLA_HEADS_PER_STEP
    kv_spec = pl.BlockSpec((1, nh, T, LANE), lambda b, h, i: (b, h, 0, 0))
    return pl.pallas_call(
        _mla_kernel,
        grid=(B, H // nh, T // tq),
        in_specs=[pl.BlockSpec((1, nh, tq, LANE), lambda b, h, i: (b, h, i, 0)), kv_spec, kv_spec],
        out_specs=pl.BlockSpec((1, tq, nh * MLA_V), lambda b, h, i: (b, i, h)),
        out_shape=jax.ShapeDtypeStruct((B, T, H * MLA_V), BF),
        compiler_params=_params("parallel", "parallel", "parallel"),
        name="mla_attention",
    )(qm, km, vm)


def _nsa_kernel(q_ref, sel_ref, ks_ref, vs_ref, kw_ref, vw_ref, oc_ref, gn_ref, o_ref, *, tk):
    tq = q_ref.shape[2]
    rows = NSA_HPG * tq
    qs = pl.program_id(2) * tq
    qa = (q_ref[0] + sel_ref[0, 0][None]).reshape(rows, LANE)
    qpos = qs + lax.broadcasted_iota(jnp.int32, (tq, 1), 0)

    def body(j, carry):
        off = pl.multiple_of(j * tk, tk)
        return _flash_step(qa, ks_ref[0, 0, pl.ds(off, tk), :], vs_ref[0, 0, pl.ds(off, tk), :], *carry)

    jd = qs // tk
    carry = (jnp.full((rows, 1), NEG, F32), jnp.zeros((rows, LANE), F32))
    m, acc = lax.fori_loop(0, jd, body, carry)
    off = pl.multiple_of(jd * tk, tk)
    kpos = off + lax.broadcasted_iota(jnp.int32, (tq, tk), 1)
    m, acc = _flash_step(qa, ks_ref[0, 0, pl.ds(off, tk), :], vs_ref[0, 0, pl.ds(off, tk), :],
                         m, acc, jnp.where(kpos <= qpos, 0.0, NEG))
    o_s = acc[:, :NSA_HD] / acc[:, ONE_LANE:ONE_LANE + 1]

    span = WINDOW + tq
    ws = pl.multiple_of(jnp.maximum(qs - WINDOW, 0), tq)
    kpos_w = ws + lax.broadcasted_iota(jnp.int32, (tq, span), 1)
    bias_w = jnp.where(kpos_w <= qpos, jnp.where(kpos_w > qpos - WINDOW, 0.0, NEG), NEG)
    _, acc_w = _flash_step(qa, kw_ref[0, 0, pl.ds(ws, span), :], vw_ref[0, 0, pl.ds(ws, span), :],
                           jnp.full((rows, 1), NEG, F32), jnp.zeros((rows, LANE), F32), bias_w)
    o_w = acc_w[:, :NSA_HD] / acc_w[:, ONE_LANE:ONE_LANE + 1]

    gn = gn_ref[0, 0]
    outs = []
    for h in range(NSA_HPG):
        r = slice(h * tq, (h + 1) * tq)
        outs.append(gn[:, 3 * h:3 * h + 1] * oc_ref[0, h]
                    + gn[:, 3 * h + 1:3 * h + 2] * o_s[r]
                    + gn[:, 3 * h + 2:3 * h + 3] * o_w[r])
    o_ref[0] = jnp.concatenate(outs, axis=1).astype(o_ref.dtype)


def _nsa_call(nq, sel, ks, vs, kw, vw, oc, gn):
    B, H, T, _ = nq.shape
    G = NSA_KV_HEADS
    tq = min(TQ_NSA, T)
    tk = min(TK_NSA, T)
    assert tk % tq == 0 and WINDOW % tq == 0 and T >= WINDOW + tq
    kv_spec = pl.BlockSpec((1, 1, T, LANE), lambda b, g, i: (b, g, 0, 0))
    grp_spec = pl.BlockSpec((1, 1, tq, LANE), lambda b, g, i: (b, g, i, 0))
    return pl.pallas_call(
        functools.partial(_nsa_kernel, tk=tk),
        grid=(B, G, T // tq),
        in_specs=[pl.BlockSpec((1, NSA_HPG, tq, LANE), lambda b, g, i: (b, g, i, 0)),
                  grp_spec, kv_spec, kv_spec, kv_spec, kv_spec,
                  pl.BlockSpec((1, NSA_HPG, tq, NSA_HD), lambda b, g, i: (b, g, i, 0)),
                  grp_spec],
        out_specs=pl.BlockSpec((1, tq, NSA_HPG * NSA_HD), lambda b, g, i: (b, i, g)),
        out_shape=jax.ShapeDtypeStruct((B, T, H * NSA_HD), BF),
        compiler_params=_params("parallel", "parallel", "parallel"),
        name="nsa_select_window",
    )(nq, sel, ks, vs, kw, vw, oc, gn)


def _merge_kernel(x_ref, om_ref, on_ref, wgm_ref, wgn_ref, wpm_ref, wpn_ref, wo_ref, g_ref, b_ref, o_ref):
    x = x_ref[...]
    xb = x.astype(BF)
    y = (jax.nn.sigmoid(_dot(xb, wgm_ref[...])) * _dot(om_ref[...], wpm_ref[...])
         + jax.nn.sigmoid(_dot(xb, wgn_ref[...])) * _dot(on_ref[...], wpn_ref[...]))
    mix = _dot(y.astype(BF), wo_ref[...])
    o_ref[...] = _ln(ALPHA * x + mix, g_ref[...], b_ref[...])


def _merge_call(x2d, om, on, w, l):
    n, d = x2d.shape
    tm = min(TM_ROWS, n)
    consts = [w["wgm"], w["wgn"], w["wpm"], w["wpn"], w["wo"], w["ln_mix_g"], w["ln_mix_b"]]
    return pl.pallas_call(
        _merge_kernel,
        grid=(n // tm,),
        in_specs=[pl.BlockSpec((tm, d), lambda i: (i, 0)),
                  pl.BlockSpec((tm, om.shape[1]), lambda i: (i, 0)),
                  pl.BlockSpec((tm, on.shape[1]), lambda i: (i, 0))]
                 + [_layer_spec(c, l) for c in consts],
        out_specs=pl.BlockSpec((tm, d), lambda i: (i, 0)),
        out_shape=jax.ShapeDtypeStruct((n, d), F32),
        compiler_params=_params("parallel"),
        name="merge_out_ln",
    )(x2d, om, on, *consts)


def _pad_last(w, width):
    return jnp.pad(w, [(0, 0)] * (w.ndim - 1) + [(0, width - w.shape[-1])])


def _stacked_weights(p):
    w_in = p["w_in"]
    nl, d = w_in.shape[:2]
    w = {}
    w["wa"] = jnp.concatenate([w_in[..., OFF_CQ:OFF_KPE], jnp.zeros((nl, d, MLA_NOPE), F32),
                               w_in[..., OFF_KPE:OFF_NQ],
                               jnp.zeros((nl, d, LANE - MLA_NOPE - MLA_ROPE), F32)], axis=-1).astype(BF)
    w["wn"] = jnp.concatenate([w_in[..., OFF_NQ:OFF_GN], _pad_last(w_in[..., OFF_GN:OFF_GM], LANE)], axis=-1).astype(BF)

    w_uq = p["w_uq"].reshape(nl, MLA_Q_RANK, MLA_HEADS, MLA_NOPE + MLA_ROPE)
    w["wq"] = _pad_last(w_uq, LANE).reshape(nl, MLA_Q_RANK, -1).astype(BF)
    pe1, pe2 = w_uq[..., MLA_NOPE:MLA_NOPE + MLA_ROPE // 2], w_uq[..., MLA_NOPE + MLA_ROPE // 2:]
    w["wqr"] = jnp.pad(jnp.concatenate([-pe2, pe1], axis=-1),
                       ((0, 0), (0, 0), (0, 0), (MLA_NOPE, LANE - MLA_NOPE - MLA_ROPE))
                       ).reshape(nl, MLA_Q_RANK, -1).astype(BF)
    w_ukv = p["w_ukv"].reshape(nl, MLA_KV_RANK, MLA_HEADS, MLA_NOPE + MLA_V)
    w["wk"] = _pad_last(w_ukv[..., :MLA_NOPE], LANE).reshape(nl, MLA_KV_RANK, -1).astype(BF)
    w["wv"] = _pad_last(w_ukv[..., MLA_NOPE:], LANE).reshape(nl, MLA_KV_RANK, -1).astype(BF)
    w["gq"] = p["q_norm_g"][:, None]
    w["gkv"] = p["kv_norm_g"][:, None]

    G = NSA_KV_HEADS
    own_group = jnp.eye(G, dtype=F32).reshape(1, G, 1, 1, G, 1, 1)
    for nm, pe, w1, b1, w2 in (("k", "cmp_pe_k", "cmp_k_w1", "cmp_k_b1", "cmp_k_w2"),
                               ("v", "cmp_pe_v", "cmp_v_w1", "cmp_v_b1", "cmp_v_w2")):
        pe_l = p[pe].reshape(nl, 2, S_CMP, 1, NSA_HD)
        w["pe" + nm] = jnp.broadcast_to(pe_l, (nl, 2, S_CMP, G, NSA_HD)).reshape(nl, 2, S_CMP * LANE)
        w1_l = p[w1].reshape(nl, 1, 2, S_CMP, 1, NSA_HD, CMP_HIDDEN)
        w["w1" + nm] = (w1_l * own_group).reshape(nl, G, 2, S_CMP * LANE, CMP_HIDDEN).astype(BF)
        w["b1" + nm] = p[b1][:, None]
        w2_l = _pad_last(p[w2], LANE).astype(BF)
        w["w2" + nm] = w2_l
        w["w2" + nm + "t"] = jnp.swapaxes(w2_l, 1, 2)

    w["wgm"] = w_in[..., OFF_GM:OFF_GNS].astype(BF)
    w["wgn"] = w_in[..., OFF_GNS:].astype(BF)
    w["wpm"] = p["w_proj_mla"].astype(BF)
    w["wpn"] = p["w_proj_nsa"].astype(BF)
    w["wo"] = p["w_out"].astype(BF)
    w["ln_mix_g"] = p["ln_mix_g"][:, None]
    w["ln_mix_b"] = p["ln_mix_b"][:, None]
    for nm in ("ffn1", "ffn2"):
        for s in ("wg", "wu", "wd"):
            w[nm + s] = p[nm + "_" + s].astype(BF)
    for nm in ("ln_f1", "ln_f2"):
        w[nm + "_g"] = p[nm + "_g"][:, None]
        w[nm + "_b"] = p[nm + "_b"][:, None]
    return w


def _rope_tables(T):
    def tables(rot_dim):
        inv = 1.0 / (ROPE_THETA ** (jnp.arange(0, rot_dim, 2, dtype=F32) / rot_dim))
        ang = jnp.arange(T, dtype=F32)[:, None] * inv[None, :]
        return jnp.cos(ang), jnp.sin(ang)

    cos_m, sin_m = tables(MLA_ROPE)
    cos_n, sin_n = tables(NSA_ROT)
    one = lambda n: jnp.ones((T, n), F32)
    zero = lambda n: jnp.zeros((T, n), F32)
    hm, hn = MLA_ROPE // 2, NSA_ROT // 2
    rest_m = LANE - MLA_NOPE - MLA_ROPE
    cm = jnp.concatenate([one(MLA_NOPE), cos_m, cos_m, one(rest_m)], 1)
    s1m = jnp.concatenate([zero(MLA_NOPE), -sin_m, zero(hm), zero(rest_m)], 1)
    s2m = jnp.concatenate([zero(MLA_NOPE), zero(hm), sin_m, zero(rest_m)], 1)
    rest_n = NSA_HD - NSA_ROT
    cn = jnp.tile(jnp.concatenate([cos_n, cos_n, one(rest_n)], 1), (1, 2))
    s1n = jnp.tile(jnp.concatenate([-sin_n, zero(hn), zero(rest_n)], 1), (1, 2))
    s2n = jnp.tile(jnp.concatenate([zero(hn), sin_n, zero(rest_n)], 1), (1, 2))
    return cm, s1m, s2m, cn, s1n, s2n


def _overlap_t(T):
    nc, ns = T // S_CMP, T // L_SLC
    n_cmp = (T - L_CMP) // S_CMP + 1
    start = np.arange(nc) * S_CMP
    blk = np.arange(ns)
    ov = ((start[None, :] < (blk[:, None] + 1) * L_SLC) & (start[None, :] + L_CMP > blk[:, None] * L_SLC)
          & (np.arange(nc)[None, :] < n_cmp))
    return jnp.asarray(ov.astype(np.float32), dtype=BF)


def kernel(x, ln_f1_g, ln_f1_b, ffn1_wg, ffn1_wu, ffn1_wd, w_in, q_norm_g, w_uq, kv_norm_g, w_ukv,
           cmp_pe_k, cmp_k_w1, cmp_k_b1, cmp_k_w2, cmp_pe_v, cmp_v_w1, cmp_v_b1, cmp_v_w2,
           w_proj_mla, w_proj_nsa, w_out, ln_mix_g, ln_mix_b,
           ffn2_wg, ffn2_wu, ffn2_wd, ln_f2_g, ln_f2_b):
    p = dict(ln_f1_g=ln_f1_g, ln_f1_b=ln_f1_b, ffn1_wg=ffn1_wg, ffn1_wu=ffn1_wu, ffn1_wd=ffn1_wd,
             w_in=w_in, q_norm_g=q_norm_g, w_uq=w_uq, kv_norm_g=kv_norm_g, w_ukv=w_ukv,
             cmp_pe_k=cmp_pe_k, cmp_k_w1=cmp_k_w1, cmp_k_b1=cmp_k_b1, cmp_k_w2=cmp_k_w2,
             cmp_pe_v=cmp_pe_v, cmp_v_w1=cmp_v_w1, cmp_v_b1=cmp_v_b1, cmp_v_w2=cmp_v_w2,
             w_proj_mla=w_proj_mla, w_proj_nsa=w_proj_nsa, w_out=w_out,
             ln_mix_g=ln_mix_g, ln_mix_b=ln_mix_b, ffn2_wg=ffn2_wg, ffn2_wu=ffn2_wu, ffn2_wd=ffn2_wd,
             ln_f2_g=ln_f2_g, ln_f2_b=ln_f2_b)
    B, T, D = x.shape
    assert T % L_SLC == 0 and T // L_SLC <= LANE - ONE_LANE
    tabs = _rope_tables(T)
    ovt = _overlap_t(T)
    n_sel = min(N_SEL, T // L_SLC)
    h = x.reshape(B * T, D)
    w = _stacked_weights(p)
    for l in range(DEPTH):
        h = _ffn_call(h, l, w["ffn1wg"], w["ffn1wu"], w["ffn1wd"], w["ln_f1_g"], w["ln_f1_b"])
        qm, km, vm, nq, kc, vc, ks, vs, kw, vw, gn = _inproj_call(h.reshape(B, T, D), w, l, tabs)
        kcc, vct = _compress_call(kc, vc, w, l)
        oc, sel = _cmp_select_call(nq, kcc, vct, ovt, n_sel)
        om = _mla_call(qm, km, vm)
        on = _nsa_call(nq, sel, ks, vs, kw, vw, oc, gn)
        h = _merge_call(h, om.reshape(B * T, -1), on.reshape(B * T, -1), w, l)
        h = _ffn_call(h, l, w["ffn2wg"], w["ffn2wu"], w["ffn2wd"], w["ln_f2_g"], w["ln_f2_b"])
    return h.reshape(B, T, D)
```

```python
import functools

import numpy as np
import jax
import jax.numpy as jnp
from jax import lax
from jax.experimental import pallas as pl
from jax.experimental.pallas import tpu as pltpu

D_MODEL = 1024
DEPTH = 2
MLA_HEADS = 8
MLA_NOPE = 64
MLA_ROPE = 32
MLA_V = 64
MLA_Q_RANK = 256
MLA_KV_RANK = 128
NSA_HEADS = 8
NSA_KV_HEADS = 2
NSA_HPG = NSA_HEADS // NSA_KV_HEADS
NSA_HD = 64
NSA_ROT = NSA_HD // 4
L_CMP = 32
S_CMP = 16
CMP_HIDDEN = 128
L_SLC = 64
LOG2_L_SLC = 6
N_SEL = 16
WINDOW = 512
ROPE_THETA = 500000.0
D_FF = 2816
ALPHA = (2 * DEPTH) ** 0.25
LN_EPS = 1e-5
RMS_EPS = 1e-6
NEG = -1e30
BIG = 1e30
LOG2E = 1.4426950408889634
MLA_SCALE = (MLA_NOPE + MLA_ROPE) ** -0.5 * LOG2E
NSA_SCALE = NSA_HD ** -0.5 * LOG2E

OFF_CQ = 0
OFF_CKV = OFF_CQ + MLA_Q_RANK
OFF_KPE = OFF_CKV + MLA_KV_RANK
OFF_NQ = OFF_KPE + MLA_ROPE
OFF_NKV = OFF_NQ + NSA_HEADS * NSA_HD
OFF_GN = OFF_NKV + 6 * NSA_KV_HEADS * NSA_HD
OFF_GM = OFF_GN + 3 * NSA_HEADS
OFF_GNS = OFF_GM + D_MODEL

LANE = 128
SUBLANE = 8
ONE_LANE = 64
VMEM_LIMIT = 56 * 1024 * 1024

BF = jnp.bfloat16
F32 = jnp.float32

MXU_DIM = 256
FF_CHUNKS = ((0, 6 * MXU_DIM), (6 * MXU_DIM, D_FF))
TM_ROWS = 512
FFN_SUBTILES = 2
TQ_MLA = 1024
MLA_HEADS_PER_STEP = 2
TQ_CMP = 256
TQ_NSA = 1024
NSA_HEADS_PER_STEP = 2
WIN_SUB = 256


def _dot(a, b):
    return jnp.dot(a, b, preferred_element_type=F32)


def _dot_nt(a, b):
    return lax.dot_general(a, b, (((1,), (1,)), ((), ())), preferred_element_type=F32)


def _ln(y, g, b):
    mu = jnp.mean(y, -1, keepdims=True)
    d = y - mu
    var = jnp.mean(d * d, -1, keepdims=True)
    return d * lax.rsqrt(var + LN_EPS) * g + b


def _params(*sem):
    return pltpu.CompilerParams(dimension_semantics=sem, vmem_limit_bytes=VMEM_LIMIT)


def _const_spec(shape):
    n = len(shape)
    return pl.BlockSpec(shape, lambda *_: (0,) * n, pipeline_mode=pl.Buffered(1))


def _layer_spec(stacked, l):
    n = stacked.ndim - 1
    return pl.BlockSpec((None,) + stacked.shape[1:], lambda *_: (l,) + (0,) * n, pipeline_mode=pl.Buffered(1))


def _ffn_ln(x, wg_ref, wu_ref, wd_ref, g, b):
    xb = x.astype(BF)
    acc = jnp.zeros(x.shape, F32)
    for c0, c1 in FF_CHUNKS:
        hg = _dot(xb, wg_ref[:, c0:c1])
        hu = _dot(xb, wu_ref[:, c0:c1])
        h = (hg * jax.nn.sigmoid(hg)) * hu
        acc = acc + _dot(h.astype(BF), wd_ref[c0:c1, :])
    return _ln(ALPHA * x + 0.5 * acc, g, b)


def _ffn_kernel(x_ref, wg_ref, wu_ref, wd_ref, g_ref, b_ref, o_ref):
    sub = x_ref.shape[0] // FFN_SUBTILES
    for r in range(FFN_SUBTILES):
        rows = slice(r * sub, (r + 1) * sub)
        o_ref[rows, :] = _ffn_ln(x_ref[rows, :], wg_ref, wu_ref, wd_ref, g_ref[...], b_ref[...])


def _ffn_call(x2d, l, wg, wu, wd, g, b):
    n, d = x2d.shape
    tm = min(TM_ROWS * FFN_SUBTILES, n)
    return pl.pallas_call(
        _ffn_kernel,
        grid=(n // tm,),
        in_specs=[pl.BlockSpec((tm, d), lambda i: (i, 0)),
                  _layer_spec(wg, l), _layer_spec(wu, l), _layer_spec(wd, l), _layer_spec(g, l), _layer_spec(b, l)],
        out_specs=pl.BlockSpec((tm, d), lambda i: (i, 0)),
        out_shape=jax.ShapeDtypeStruct((n, d), F32),
        compiler_params=_params("parallel"),
        name="ffn_ln",
    )(x2d, wg, wu, wd, g, b)


def _rope(v, c, s1, s2, half):
    return v * c + pltpu.roll(v, LANE - half, 1) * s1 + pltpu.roll(v, half, 1) * s2


def _rms(v, g):
    return v * lax.rsqrt(jnp.mean(v * v, -1, keepdims=True) + RMS_EPS) * g


def _inproj_kernel(x_ref, wa_ref, wn_ref, wq_ref, wqr_ref, wk_ref, wv_ref, gq_ref, gkv_ref,
                   cm_ref, s1m_ref, s2m_ref, cn_ref, s1n_ref, s2n_ref,
                   qm_ref, km_ref, vm_ref, nq_ref, kc_ref, vc_ref,
                   ks_ref, vs_ref, kw_ref, vw_ref, gn_ref):
    tm = x_ref.shape[1]
    t0 = pl.program_id(1) * tm
    xb = x_ref[0].astype(BF)
    lane = lax.broadcasted_iota(jnp.int32, (tm, LANE), 1)
    row = lax.broadcasted_iota(jnp.int32, (tm, LANE), 0) + t0
    one_col = (lane == ONE_LANE).astype(F32)
    blk_onehot = (lane - ONE_LANE == (row >> LOG2_L_SLC)).astype(F32)
    cm, s1m, s2m = cm_ref[...], s1m_ref[...], s2m_ref[...]
    cn, s1n, s2n = cn_ref[...], s1n_ref[...], s2n_ref[...]
    hm = MLA_ROPE // 2
    hn = NSA_ROT // 2

    pa = _dot(xb, wa_ref[...])
    pn = _dot(xb, wn_ref[...])
    cqn = _rms(pa[:, :MLA_Q_RANK], gq_ref[...]).astype(BF)
    ckvn = _rms(pa[:, MLA_Q_RANK:MLA_Q_RANK + MLA_KV_RANK], gkv_ref[...]).astype(BF)
    kpe = _rope(pa[:, MLA_Q_RANK + MLA_KV_RANK:], cm, s1m, s2m, hm)
    q = _dot(cqn, wq_ref[...])
    q_rot = _dot(cqn, wqr_ref[...])
    kn = _dot(ckvn, wk_ref[...])
    vv = _dot(ckvn, wv_ref[...])
    c_q, s_q = cm * MLA_SCALE, (s2m - s1m) * MLA_SCALE
    for h in range(MLA_HEADS):
        sl = slice(h * LANE, (h + 1) * LANE)
        qm_ref[0, h] = (q[:, sl] * c_q + q_rot[:, sl] * s_q).astype(BF)
        km_ref[0, h] = (kn[:, sl] + kpe).astype(BF)
        vm_ref[0, h] = (vv[:, sl] + one_col).astype(BF)

    low = lane < NSA_HD

    def halves(v):
        return jnp.where(low, v, 0.0), jnp.where(low, pltpu.roll(v, NSA_HD, 1), 0.0)

    def slab(i):
        return pn[:, i * LANE:(i + 1) * LANE]

    for i in range(NSA_HEADS // 2):
        for j, qh in enumerate(halves(_rope(slab(i), cn, s1n, s2n, hn) * NSA_SCALE)):
            nq_ref[0, 2 * i + j] = qh.astype(BF)
    o = NSA_HEADS // 2
    kc_ref[0] = _rope(slab(o), cn, s1n, s2n, hn)
    vc_ref[0] = slab(o + 1)
    k_s = halves(_rope(slab(o + 2), cn, s1n, s2n, hn))
    v_s = halves(slab(o + 3))
    k_w = halves(_rope(slab(o + 4), cn, s1n, s2n, hn))
    v_w = halves(slab(o + 5))
    gn_ref[0] = jax.nn.sigmoid(slab(o + 6))
    for g in range(NSA_KV_HEADS):
        ks_ref[0, g] = (k_s[g] + blk_onehot).astype(BF)
        vs_ref[0, g] = (v_s[g] + one_col).astype(BF)
        kw_ref[0, g] = k_w[g].astype(BF)
        vw_ref[0, g] = (v_w[g] + one_col).astype(BF)


def _inproj_call(x, w, l, tabs):
    B, T, D = x.shape
    tm = min(TM_ROWS, T)
    H, G = MLA_HEADS, NSA_KV_HEADS
    head_spec = pl.BlockSpec((1, H, tm, LANE), lambda b, i: (b, 0, i, 0))
    grp_spec = pl.BlockSpec((1, G, tm, LANE), lambda b, i: (b, 0, i, 0))
    row_spec = pl.BlockSpec((1, tm, LANE), lambda b, i: (b, i, 0))
    tab_spec = pl.BlockSpec((tm, LANE), lambda b, i: (i, 0))
    head_shape = jax.ShapeDtypeStruct((B, H, T, LANE), BF)
    grp_shape = jax.ShapeDtypeStruct((B, G, T, LANE), BF)
    consts = [w["wa"], w["wn"], w["wq"], w["wqr"], w["wk"], w["wv"], w["gq"], w["gkv"]]
    return pl.pallas_call(
        _inproj_kernel,
        grid=(B, T // tm),
        in_specs=[pl.BlockSpec((1, tm, D), lambda b, i: (b, i, 0))]
                 + [_layer_spec(c, l) for c in consts] + [tab_spec] * 6,
        out_specs=[head_spec] * 4 + [row_spec] * 2 + [grp_spec] * 4 + [row_spec],
        out_shape=[head_shape] * 4 + [jax.ShapeDtypeStruct((B, T, LANE), F32)] * 2
                  + [grp_shape] * 4 + [jax.ShapeDtypeStruct((B, T, LANE), F32)],
        compiler_params=_params("parallel", "parallel"),
        name="in_proj",
    )(x, *consts, *tabs)


def _gelu_tanh(x):
    return 0.5 * x * (1.0 + jnp.tanh(np.sqrt(2.0 / np.pi) * (x + 0.044715 * (x * x * x))))


def _compress_kernel(kc_ref, vc_ref, pek_ref, pev_ref, w1k_ref, w1v_ref, b1k_ref, b1v_ref,
                     w2k_ref, w2vt_ref, kco_ref, vcto_ref):
    nc = kc_ref.shape[1]
    one_row = (lax.broadcasted_iota(jnp.int32, (LANE, nc), 0) == ONE_LANE).astype(F32)
    for src, pe, w1, b1, w2, dst, transposed in (
            (kc_ref, pek_ref, w1k_ref, b1k_ref, w2k_ref, kco_ref, False),
            (vc_ref, pev_ref, w1v_ref, b1v_ref, w2vt_ref, vcto_ref, True)):
        r = src[0]
        a_lo = (r + pe[0:1]).astype(BF)
        a_hi = (r + pe[1:2]).astype(BF)
        for g in range(NSA_KV_HEADS):
            h_hi = _dot(a_hi, w1[g, 1])
            h = _dot(a_lo, w1[g, 0]) + pltpu.roll(h_hi, nc - 1, 0) + b1[...]
            act = _gelu_tanh(h).astype(BF)
            if transposed:
                dst[0, g] = (_dot_nt(w2[...], act) + one_row).astype(BF)
            else:
                dst[0, g] = _dot(act, w2[...]).astype(BF)


def _compress_call(kc, vc, w, l):
    B, T, _ = kc.shape
    nc = T // S_CMP
    width = S_CMP * LANE
    kc2 = kc.reshape(B, nc, width)
    vc2 = vc.reshape(B, nc, width)
    consts = [w["pek"], w["pev"], w["w1k"], w["w1v"], w["b1k"], w["b1v"], w["w2k"], w["w2vt"]]
    in_spec = pl.BlockSpec((1, nc, width), lambda b: (b, 0, 0))
    return pl.pallas_call(
        _compress_kernel,
        grid=(B,),
        in_specs=[in_spec, in_spec] + [_layer_spec(c, l) for c in consts],
        out_specs=[pl.BlockSpec((1, NSA_KV_HEADS, nc, LANE), lambda b: (b, 0, 0, 0)),
                   pl.BlockSpec((1, NSA_KV_HEADS, LANE, nc), lambda b: (b, 0, 0, 0))],
        out_shape=[jax.ShapeDtypeStruct((B, NSA_KV_HEADS, nc, LANE), BF),
                   jax.ShapeDtypeStruct((B, NSA_KV_HEADS, LANE, nc), BF)],
        compiler_params=_params("parallel"),
        name="nsa_compress",
    )(kc2, vc2, *consts)


def _cmp_select_kernel(q_ref, kc_ref, vct_ref, ovt_ref, oc_ref, sel_ref, imp_ref, cnt_ref, *, n_sel):
    tq = q_ref.shape[2]
    nc = kc_ref.shape[2]
    ns = ovt_ref.shape[0]
    qs = pl.program_id(2) * tq
    kc = kc_ref[0, 0]
    vct = vct_ref[0, 0]

    t_c = lax.broadcasted_iota(jnp.int32, (nc, tq), 1) + qs
    c_c = lax.broadcasted_iota(jnp.int32, (nc, tq), 0)
    bias = jnp.where(c_c * S_CMP + (L_CMP - 1) <= t_c, 0.0, NEG)
    has_valid = jnp.where(lax.broadcasted_iota(jnp.int32, (1, tq), 1) + qs >= L_CMP - 1, 1.0, 0.0)
    bias = jnp.concatenate([bias] * NSA_HPG, axis=1)
    has_valid = jnp.concatenate([has_valid] * NSA_HPG, axis=1)
    st = _dot_nt(kc, q_ref[0].reshape(NSA_HPG * tq, LANE)) + bias
    et = jnp.exp2(st - st.max(0, keepdims=True))
    p = et * (has_valid / et.sum(0, keepdims=True))
    acc_t = _dot(vct, et.astype(BF))
    o_t = acc_t * (has_valid / acc_t[ONE_LANE:ONE_LANE + 1, :])
    p_sum = p[:, :tq]
    for h in range(1, NSA_HPG):
        p_sum = p_sum + p[:, h * tq:(h + 1) * tq]
    oc_ref[0] = jnp.concatenate([o_t[:, h * tq:(h + 1) * tq].T[:, :NSA_HD] for h in range(NSA_HPG)], axis=1)

    hi = p_sum.astype(BF)
    lo = (p_sum - hi.astype(F32)).astype(BF)
    imp = _dot(ovt_ref[...], hi) + _dot(ovt_ref[...], lo)
    j = lax.broadcasted_iota(jnp.int32, (ns, tq), 0)
    cur = (lax.broadcasted_iota(jnp.int32, (ns, tq), 1) + qs) >> LOG2_L_SLC
    forced = (j == 0) | (j == cur) | (j == cur - 1)
    imp_ref[...] = jnp.where(forced, BIG, jnp.where(j <= cur, imp, NEG))
    cnt_ref[...] = jnp.zeros((ns, tq), F32)
    cur_max = (qs + tq - 1) >> LOG2_L_SLC
    n_grp = ns // SUBLANE
    j_in = lax.broadcasted_iota(jnp.int32, (SUBLANE, tq), 0)
    for kg in range(n_grp):
        @pl.when((kg * SUBLANE <= cur_max) & (cur_max >= n_sel))
        def _():
            grps = [imp_ref[r * SUBLANE:(r + 1) * SUBLANE, :] for r in range(n_grp)]
            cnts = [cnt_ref[r * SUBLANE:(r + 1) * SUBLANE, :] for r in range(n_grp)]
            for k in range(kg * SUBLANE, (kg + 1) * SUBLANE):
                row_k = imp_ref[k:k + 1, :]
                for r, grp in enumerate(grps):
                    if r > kg:
                        ahead = jnp.where(row_k >= grp, 1.0, 0.0)
                    elif r < kg:
                        ahead = jnp.where(row_k > grp, 1.0, 0.0)
                    else:
                        ahead = jnp.where(j_in > k % SUBLANE, jnp.where(row_k >= grp, 1.0, 0.0),
                                          jnp.where(row_k > grp, 1.0, 0.0))
                    cnts[r] = cnts[r] + ahead
            for r in range(n_grp):
                cnt_ref[r * SUBLANE:(r + 1) * SUBLANE, :] = cnts[r]
    not_sel = jnp.where(cnt_ref[...] < n_sel, 0.0, 1.0)
    pieces = [jnp.zeros((ONE_LANE, tq), F32), not_sel]
    if ns < LANE - ONE_LANE:
        pieces.append(jnp.zeros((LANE - ONE_LANE - ns, tq), F32))
    sel_ref[0, 0] = (jnp.concatenate(pieces, axis=0).T * NEG).astype(BF)


def _cmp_select_call(nq, kcc, vct, ovt, n_sel):
    B, H, T, _ = nq.shape
    G = NSA_KV_HEADS
    nc = kcc.shape[2]
    ns = ovt.shape[0]
    tq = min(TQ_CMP, T)
    return pl.pallas_call(
        functools.partial(_cmp_select_kernel, n_sel=n_sel),
        grid=(B, G, T // tq),
        in_specs=[pl.BlockSpec((1, NSA_HPG, tq, LANE), lambda b, g, i: (b, g, i, 0)),
                  pl.BlockSpec((1, 1, nc, LANE), lambda b, g, i: (b, g, 0, 0)),
                  pl.BlockSpec((1, 1, LANE, nc), lambda b, g, i: (b, g, 0, 0)),
                  _const_spec(ovt.shape)],
        out_specs=[pl.BlockSpec((1, tq, NSA_HPG * NSA_HD), lambda b, g, i: (b, i, g)),
                   pl.BlockSpec((1, 1, tq, LANE), lambda b, g, i: (b, g, i, 0))],
        out_shape=[jax.ShapeDtypeStruct((B, T, H * NSA_HD), F32),
                   jax.ShapeDtypeStruct((B, G, T, LANE), BF)],
        scratch_shapes=[pltpu.VMEM((ns, tq), F32), pltpu.VMEM((ns, tq), F32)],
        compiler_params=_params("parallel", "parallel", "parallel"),
        name="nsa_cmp_select",
    )(nq, kcc, vct, ovt)


def _flash_step(q, k, v, m, acc, bias=None):
    s = _dot_nt(q, k)
    if bias is not None:
        reps = s.shape[0] // bias.shape[0]
        s = s + bias if reps == 1 else (s.reshape(reps, *bias.shape) + bias[None]).reshape(s.shape)
    m_new = jnp.maximum(m, s.max(-1, keepdims=True))
    p = jnp.exp2(s - m_new)
    acc = acc * jnp.exp2(m - m_new) + _dot(p.astype(BF), v)
    return m_new, acc


def _mla_kernel(q_ref, k_ref, v_ref, o_ref):
    nh, tq = q_ref.shape[1], q_ref.shape[2]
    qi = pl.program_id(2)
    half = tq // 2
    causal_bias = jnp.where(lax.broadcasted_iota(jnp.int32, (half, half), 1)
                            <= lax.broadcasted_iota(jnp.int32, (half, half), 0), 0.0, NEG)
    lower_bias = jnp.concatenate([jnp.zeros((half, half), F32), causal_bias], axis=1)

    def step(off, carry, bias):
        out = []
        for hh in range(nh):
            out += _flash_step(q_ref[0, hh], k_ref[0, hh, pl.ds(off, tq), :], v_ref[0, hh, pl.ds(off, tq), :],
                               carry[2 * hh], carry[2 * hh + 1], bias)
        return tuple(out)

    carry = (jnp.full((tq, 1), NEG, F32), jnp.zeros((tq, LANE), F32)) * nh
    carry = lax.fori_loop(0, qi, lambda j, c: step(pl.multiple_of(j * tq, tq), c, None), carry)
    off = pl.multiple_of(qi * tq, tq)
    outs = []
    for hh in range(nh):
        m, acc = carry[2 * hh], carry[2 * hh + 1]
        _, acc_t = _flash_step(q_ref[0, hh, :half, :], k_ref[0, hh, pl.ds(off, half), :],
                               v_ref[0, hh, pl.ds(off, half), :], m[:half], acc[:half], causal_bias)
        _, acc_b = _flash_step(q_ref[0, hh, half:, :], k_ref[0, hh, pl.ds(off, tq), :],
                               v_ref[0, hh, pl.ds(off, tq), :], m[half:], acc[half:], lower_bias)
        acc = jnp.concatenate([acc_t, acc_b], axis=0)
        outs.append(acc[:, :MLA_V] / acc[:, ONE_LANE:ONE_LANE + 1])
    o_ref[0] = jnp.concatenate(outs, axis=1).astype(o_ref.dtype)


def _mla_call(qm, km, vm):
    B, H, T, _ = qm.shape
    tq = min(TQ_MLA, T)
    nh = MLA_HEADS_PER_STEP
    kv_spec = pl.BlockSpec((1, nh, T, LANE), lambda b, h, i: (b, h, 0, 0))
    return pl.pallas_call(
        _mla_kernel,
        grid=(B, H // nh, T // tq),
        in_specs=[pl.BlockSpec((1, nh, tq, LANE), lambda b, h, i: (b, h, i, 0)), kv_spec, kv_spec],
        out_specs=pl.BlockSpec((1, tq, nh * MLA_V), lambda b, h, i: (b, i, h)),
        out_shape=jax.ShapeDtypeStruct((B, T, H * MLA_V), BF),
        compiler_params=_params("parallel", "parallel", "parallel"),
        name="mla_attention",
    )(qm, km, vm)


def _nsa_kernel(q_ref, sel_ref, ks_ref, vs_ref, kw_ref, vw_ref, os_ref, ow_ref):
    nh, tq = q_ref.shape[1], q_ref.shape[2]
    qi = pl.program_id(2)
    qs = qi * tq
    half = tq // 2
    causal_bias = jnp.where(lax.broadcasted_iota(jnp.int32, (half, half), 1)
                            <= lax.broadcasted_iota(jnp.int32, (half, half), 0), 0.0, NEG)
    lower_bias = jnp.concatenate([jnp.zeros((half, half), F32), causal_bias], axis=1)

    def qa(hh, rows=slice(None)):
        return q_ref[0, hh, rows, :] + sel_ref[0, 0, rows, :]

    def step(off, carry):
        out = []
        for hh in range(nh):
            out += _flash_step(qa(hh), ks_ref[0, 0, pl.ds(off, tq), :], vs_ref[0, 0, pl.ds(off, tq), :],
                               carry[2 * hh], carry[2 * hh + 1])
        return tuple(out)

    carry = (jnp.full((tq, 1), NEG, F32), jnp.zeros((tq, LANE), F32)) * nh
    carry = lax.fori_loop(0, qi, lambda j, c: step(pl.multiple_of(j * tq, tq), c), carry)
    off = pl.multiple_of(qs, tq)
    o_s = []
    for hh in range(nh):
        m, acc = carry[2 * hh], carry[2 * hh + 1]
        _, acc_t = _flash_step(qa(hh, slice(0, half)), ks_ref[0, 0, pl.ds(off, half), :],
                               vs_ref[0, 0, pl.ds(off, half), :], m[:half], acc[:half], causal_bias)
        _, acc_b = _flash_step(qa(hh, slice(half, tq)), ks_ref[0, 0, pl.ds(off, tq), :],
                               vs_ref[0, 0, pl.ds(off, tq), :], m[half:], acc[half:], lower_bias)
        acc = jnp.concatenate([acc_t, acc_b], axis=0)
        o_s.append(acc[:, :NSA_HD] / acc[:, ONE_LANE:ONE_LANE + 1])
    os_ref[0] = jnp.concatenate(o_s, axis=1)

    sub = min(WIN_SUB, tq)
    span = WINDOW + sub
    for u in range(tq // sub):
        rows = slice(u * sub, (u + 1) * sub)
        qs_u = qs + u * sub
        qpos = qs_u + lax.broadcasted_iota(jnp.int32, (sub, 1), 0)
        ws = pl.multiple_of(jnp.maximum(qs_u - WINDOW, 0), sub)
        kpos = ws + lax.broadcasted_iota(jnp.int32, (sub, span), 1)
        bias_w = jnp.where(kpos <= qpos, jnp.where(kpos > qpos - WINDOW, 0.0, NEG), NEG)
        q_u = jnp.concatenate([qa(hh, rows) for hh in range(nh)], axis=0)
        _, acc_w = _flash_step(q_u, kw_ref[0, 0, pl.ds(ws, span), :], vw_ref[0, 0, pl.ds(ws, span), :],
                               jnp.full((nh * sub, 1), NEG, F32), jnp.zeros((nh * sub, LANE), F32), bias_w)
        o_u = acc_w[:, :NSA_HD] / acc_w[:, ONE_LANE:ONE_LANE + 1]
        ow_ref[0, rows, :] = jnp.concatenate([o_u[hh * sub:(hh + 1) * sub] for hh in range(nh)], axis=1)


def _nsa_call(nq, sel, ks, vs, kw, vw):
    B, H, T, _ = nq.shape
    nh = NSA_HEADS_PER_STEP
    per_grp = NSA_HPG // nh
    tq = min(TQ_NSA, T)
    sub = min(WIN_SUB, tq)
    assert WINDOW % sub == 0 and tq % sub == 0 and T >= WINDOW + sub
    kv_spec = pl.BlockSpec((1, 1, T, LANE), lambda b, h, i: (b, h // per_grp, 0, 0))
    out_spec = pl.BlockSpec((1, tq, nh * NSA_HD), lambda b, h, i: (b, i, h))
    out_shape = jax.ShapeDtypeStruct((B, T, H * NSA_HD), F32)
    return pl.pallas_call(
        _nsa_kernel,
        grid=(B, H // nh, T // tq),
        in_specs=[pl.BlockSpec((1, nh, tq, LANE), lambda b, h, i: (b, h, i, 0)),
                  pl.BlockSpec((1, 1, tq, LANE), lambda b, h, i: (b, h // per_grp, i, 0)),
                  kv_spec, kv_spec, kv_spec, kv_spec],
        out_specs=[out_spec, out_spec],
        out_shape=[out_shape, out_shape],
        compiler_params=_params("parallel", "parallel", "parallel"),
        name="nsa_select_window",
    )(nq, sel, ks, vs, kw, vw)


def _merge_kernel(x_ref, om_ref, oc_ref, os_ref, ow_ref, gb_ref, wgm_ref, wgn_ref, wpm_ref, wpn_ref, wo_ref,
                  g_ref, b_ref, o_ref):
    x = x_ref[...]
    xb = x.astype(BF)
    gates = gb_ref[...]
    tm = gates.shape[0]
    head = lax.broadcasted_iota(jnp.int32, (tm, LANE), 1) // NSA_HD
    slabs = []
    for j in range(NSA_HEADS * NSA_HD // LANE):
        sl = slice(j * LANE, (j + 1) * LANE)
        idx = 3 * (head + j * (LANE // NSA_HD))
        slabs.append(jnp.take_along_axis(gates, idx, axis=1) * oc_ref[:, sl]
                     + jnp.take_along_axis(gates, idx + 1, axis=1) * os_ref[:, sl]
                     + jnp.take_along_axis(gates, idx + 2, axis=1) * ow_ref[:, sl])
    o_n = jnp.concatenate(slabs, axis=1).astype(BF)
    y = (jax.nn.sigmoid(_dot(xb, wgm_ref[...])) * _dot(om_ref[...], wpm_ref[...])
         + jax.nn.sigmoid(_dot(xb, wgn_ref[...])) * _dot(o_n, wpn_ref[...]))
    mix = _dot(y.astype(BF), wo_ref[...])
    o_ref[...] = _ln(ALPHA * x + mix, g_ref[...], b_ref[...])


def _merge_call(x2d, om, oc, o_s, o_w, gates, w, l):
    n, d = x2d.shape
    tm = min(TM_ROWS, n)
    row = lambda a: pl.BlockSpec((tm, a.shape[1]), lambda i: (i, 0))
    consts = [w["wgm"], w["wgn"], w["wpm"], w["wpn"], w["wo"], w["ln_mix_g"], w["ln_mix_b"]]
    return pl.pallas_call(
        _merge_kernel,
        grid=(n // tm,),
        in_specs=[row(a) for a in (x2d, om, oc, o_s, o_w, gates)] + [_layer_spec(c, l) for c in consts],
        out_specs=pl.BlockSpec((tm, d), lambda i: (i, 0)),
        out_shape=jax.ShapeDtypeStruct((n, d), F32),
        compiler_params=_params("parallel"),
        name="merge_out_ln",
    )(x2d, om, oc, o_s, o_w, gates, *consts)


def _pad_last(w, width):
    return jnp.pad(w, [(0, 0)] * (w.ndim - 1) + [(0, width - w.shape[-1])])


def _stacked_weights(p):
    w_in = p["w_in"]
    nl, d = w_in.shape[:2]
    w = {}
    w["wa"] = jnp.concatenate([w_in[..., OFF_CQ:OFF_KPE], jnp.zeros((nl, d, MLA_NOPE), F32),
                               w_in[..., OFF_KPE:OFF_NQ],
                               jnp.zeros((nl, d, LANE - MLA_NOPE - MLA_ROPE), F32)], axis=-1).astype(BF)
    w["wn"] = jnp.concatenate([w_in[..., OFF_NQ:OFF_GN], _pad_last(w_in[..., OFF_GN:OFF_GM], LANE)], axis=-1).astype(BF)

    w_uq = p["w_uq"].reshape(nl, MLA_Q_RANK, MLA_HEADS, MLA_NOPE + MLA_ROPE)
    w["wq"] = _pad_last(w_uq, LANE).reshape(nl, MLA_Q_RANK, -1).astype(BF)
    pe1, pe2 = w_uq[..., MLA_NOPE:MLA_NOPE + MLA_ROPE // 2], w_uq[..., MLA_NOPE + MLA_ROPE // 2:]
    w["wqr"] = jnp.pad(jnp.concatenate([-pe2, pe1], axis=-1),
                       ((0, 0), (0, 0), (0, 0), (MLA_NOPE, LANE - MLA_NOPE - MLA_ROPE))
                       ).reshape(nl, MLA_Q_RANK, -1).astype(BF)
    w_ukv = p["w_ukv"].reshape(nl, MLA_KV_RANK, MLA_HEADS, MLA_NOPE + MLA_V)
    w["wk"] = _pad_last(w_ukv[..., :MLA_NOPE], LANE).reshape(nl, MLA_KV_RANK, -1).astype(BF)
    w["wv"] = _pad_last(w_ukv[..., MLA_NOPE:], LANE).reshape(nl, MLA_KV_RANK, -1).astype(BF)
    w["gq"] = p["q_norm_g"][:, None]
    w["gkv"] = p["kv_norm_g"][:, None]

    G = NSA_KV_HEADS
    own_group = jnp.eye(G, dtype=F32).reshape(1, G, 1, 1, G, 1, 1)
    for nm, pe, w1, b1, w2 in (("k", "cmp_pe_k", "cmp_k_w1", "cmp_k_b1", "cmp_k_w2"),
                               ("v", "cmp_pe_v", "cmp_v_w1", "cmp_v_b1", "cmp_v_w2")):
        pe_l = p[pe].reshape(nl, 2, S_CMP, 1, NSA_HD)
        w["pe" + nm] = jnp.broadcast_to(pe_l, (nl, 2, S_CMP, G, NSA_HD)).reshape(nl, 2, S_CMP * LANE)
        w1_l = p[w1].reshape(nl, 1, 2, S_CMP, 1, NSA_HD, CMP_HIDDEN)
        w["w1" + nm] = (w1_l * own_group).reshape(nl, G, 2, S_CMP * LANE, CMP_HIDDEN).astype(BF)
        w["b1" + nm] = p[b1][:, None]
        w2_l = _pad_last(p[w2], LANE).astype(BF)
        w["w2" + nm] = w2_l
        w["w2" + nm + "t"] = jnp.swapaxes(w2_l, 1, 2)

    w["wgm"] = w_in[..., OFF_GM:OFF_GNS].astype(BF)
    w["wgn"] = w_in[..., OFF_GNS:].astype(BF)
    w["wpm"] = p["w_proj_mla"].astype(BF)
    w["wpn"] = p["w_proj_nsa"].astype(BF)
    w["wo"] = p["w_out"].astype(BF)
    w["ln_mix_g"] = p["ln_mix_g"][:, None]
    w["ln_mix_b"] = p["ln_mix_b"][:, None]
    for nm in ("ffn1", "ffn2"):
        for s in ("wg", "wu", "wd"):
            w[nm + s] = p[nm + "_" + s].astype(BF)
    for nm in ("ln_f1", "ln_f2"):
        w[nm + "_g"] = p[nm + "_g"][:, None]
        w[nm + "_b"] = p[nm + "_b"][:, None]
    return w


def _rope_tables(T):
    def tables(rot_dim):
        inv = 1.0 / (ROPE_THETA ** (jnp.arange(0, rot_dim, 2, dtype=F32) / rot_dim))
        ang = jnp.arange(T, dtype=F32)[:, None] * inv[None, :]
        return jnp.cos(ang), jnp.sin(ang)

    cos_m, sin_m = tables(MLA_ROPE)
    cos_n, sin_n = tables(NSA_ROT)
    one = lambda n: jnp.ones((T, n), F32)
    zero = lambda n: jnp.zeros((T, n), F32)
    hm, hn = MLA_ROPE // 2, NSA_ROT // 2
    rest_m = LANE - MLA_NOPE - MLA_ROPE
    cm = jnp.concatenate([one(MLA_NOPE), cos_m, cos_m, one(rest_m)], 1)
    s1m = jnp.concatenate([zero(MLA_NOPE), -sin_m, zero(hm), zero(rest_m)], 1)
    s2m = jnp.concatenate([zero(MLA_NOPE), zero(hm), sin_m, zero(rest_m)], 1)
    rest_n = NSA_HD - NSA_ROT
    cn = jnp.tile(jnp.concatenate([cos_n, cos_n, one(rest_n)], 1), (1, 2))
    s1n = jnp.tile(jnp.concatenate([-sin_n, zero(hn), zero(rest_n)], 1), (1, 2))
    s2n = jnp.tile(jnp.concatenate([zero(hn), sin_n, zero(rest_n)], 1), (1, 2))
    return cm, s1m, s2m, cn, s1n, s2n


def _overlap_t(T):
    nc, ns = T // S_CMP, T // L_SLC
    n_cmp = (T - L_CMP) // S_CMP + 1
    start = np.arange(nc) * S_CMP
    blk = np.arange(ns)
    ov = ((start[None, :] < (blk[:, None] + 1) * L_SLC) & (start[None, :] + L_CMP > blk[:, None] * L_SLC)
          & (np.arange(nc)[None, :] < n_cmp))
    return jnp.asarray(ov.astype(np.float32), dtype=BF)


def kernel(x, ln_f1_g, ln_f1_b, ffn1_wg, ffn1_wu, ffn1_wd, w_in, q_norm_g, w_uq, kv_norm_g, w_ukv,
           cmp_pe_k, cmp_k_w1, cmp_k_b1, cmp_k_w2, cmp_pe_v, cmp_v_w1, cmp_v_b1, cmp_v_w2,
           w_proj_mla, w_proj_nsa, w_out, ln_mix_g, ln_mix_b,
           ffn2_wg, ffn2_wu, ffn2_wd, ln_f2_g, ln_f2_b):
    p = dict(ln_f1_g=ln_f1_g, ln_f1_b=ln_f1_b, ffn1_wg=ffn1_wg, ffn1_wu=ffn1_wu, ffn1_wd=ffn1_wd,
             w_in=w_in, q_norm_g=q_norm_g, w_uq=w_uq, kv_norm_g=kv_norm_g, w_ukv=w_ukv,
             cmp_pe_k=cmp_pe_k, cmp_k_w1=cmp_k_w1, cmp_k_b1=cmp_k_b1, cmp_k_w2=cmp_k_w2,
             cmp_pe_v=cmp_pe_v, cmp_v_w1=cmp_v_w1, cmp_v_b1=cmp_v_b1, cmp_v_w2=cmp_v_w2,
             w_proj_mla=w_proj_mla, w_proj_nsa=w_proj_nsa, w_out=w_out,
             ln_mix_g=ln_mix_g, ln_mix_b=ln_mix_b, ffn2_wg=ffn2_wg, ffn2_wu=ffn2_wu, ffn2_wd=ffn2_wd,
             ln_f2_g=ln_f2_g, ln_f2_b=ln_f2_b)
    B, T, D = x.shape
    assert T % L_SLC == 0 and T // L_SLC <= LANE - ONE_LANE
    tabs = _rope_tables(T)
    ovt = _overlap_t(T)
    n_sel = min(N_SEL, T // L_SLC)
    h = x.reshape(B * T, D)
    w = _stacked_weights(p)
    for l in range(DEPTH):
        h = _ffn_call(h, l, w["ffn1wg"], w["ffn1wu"], w["ffn1wd"], w["ln_f1_g"], w["ln_f1_b"])
        qm, km, vm, nq, kc, vc, ks, vs, kw, vw, gn = _inproj_call(h.reshape(B, T, D), w, l, tabs)
        kcc, vct = _compress_call(kc, vc, w, l)
        oc, sel = _cmp_select_call(nq, kcc, vct, ovt, n_sel)
        om = _mla_call(qm, km, vm)
        o_s, o_w = _nsa_call(nq, sel, ks, vs, kw, vw)
        flat = lambda a: a.reshape(B * T, -1)
        h = _merge_call(h, flat(om), flat(oc), flat(o_s), flat(o_w), flat(gn), w, l)
        h = _ffn_call(h, l, w["ffn2wg"], w["ffn2wu"], w["ffn2wd"], w["ln_f2_g"], w["ln_f2_b"])
    return h.reshape(B, T, D)
```

```python
import functools

import numpy as np
import jax
import jax.numpy as jnp
from jax import lax
from jax.experimental import pallas as pl
from jax.experimental.pallas import tpu as pltpu

D_MODEL = 1024
DEPTH = 2
MLA_HEADS = 8
MLA_NOPE = 64
MLA_ROPE = 32
MLA_V = 64
MLA_Q_RANK = 256
MLA_KV_RANK = 128
NSA_HEADS = 8
NSA_KV_HEADS = 2
NSA_HPG = NSA_HEADS // NSA_KV_HEADS
NSA_HD = 64
NSA_ROT = NSA_HD // 4
L_CMP = 32
S_CMP = 16
CMP_HIDDEN = 128
L_SLC = 64
LOG2_L_SLC = 6
N_SEL = 16
WINDOW = 512
ROPE_THETA = 500000.0
D_FF = 2816
ALPHA = (2 * DEPTH) ** 0.25
LN_EPS = 1e-5
RMS_EPS = 1e-6
NEG = -1e30
BIG = 1e30
LOG2E = 1.4426950408889634
MLA_SCALE = (MLA_NOPE + MLA_ROPE) ** -0.5 * LOG2E
NSA_SCALE = NSA_HD ** -0.5 * LOG2E

OFF_CQ = 0
OFF_CKV = OFF_CQ + MLA_Q_RANK
OFF_KPE = OFF_CKV + MLA_KV_RANK
OFF_NQ = OFF_KPE + MLA_ROPE
OFF_NKV = OFF_NQ + NSA_HEADS * NSA_HD
OFF_GN = OFF_NKV + 6 * NSA_KV_HEADS * NSA_HD
OFF_GM = OFF_GN + 3 * NSA_HEADS
OFF_GNS = OFF_GM + D_MODEL

LANE = 128
SUBLANE = 8
ONE_LANE = 64
VMEM_LIMIT = 56 * 1024 * 1024

BF = jnp.bfloat16
F32 = jnp.float32

MXU_DIM = 256
FF_CHUNKS = ((0, 6 * MXU_DIM), (6 * MXU_DIM, D_FF))
TM_ROWS = 512
FFN_SUBTILES = 2
TQ_MLA = 1024
MLA_HEADS_PER_STEP = 2
TQ_CMP = 256
TQ_NSA = 1024
NSA_HEADS_PER_STEP = 2
WIN_SUB = 256


def _dot(a, b):
    return jnp.dot(a, b, preferred_element_type=F32)


def _dot_nt(a, b):
    return lax.dot_general(a, b, (((1,), (1,)), ((), ())), preferred_element_type=F32)


def _ln(y, g, b):
    mu = jnp.mean(y, -1, keepdims=True)
    d = y - mu
    var = jnp.mean(d * d, -1, keepdims=True)
    return d * lax.rsqrt(var + LN_EPS) * g + b


def _params(*sem):
    return pltpu.CompilerParams(dimension_semantics=sem, vmem_limit_bytes=VMEM_LIMIT)


def _const_spec(shape):
    n = len(shape)
    return pl.BlockSpec(shape, lambda *_: (0,) * n, pipeline_mode=pl.Buffered(1))


def _layer_spec(stacked, l):
    n = stacked.ndim - 1
    return pl.BlockSpec((None,) + stacked.shape[1:], lambda *_: (l,) + (0,) * n, pipeline_mode=pl.Buffered(1))


def _ffn_ln(x, wg_ref, wu_ref, wd_ref, g, b):
    xb = x.astype(BF)
    acc = jnp.zeros(x.shape, F32)
    for c0, c1 in FF_CHUNKS:
        hg = _dot(xb, wg_ref[:, c0:c1])
        hu = _dot(xb, wu_ref[:, c0:c1])
        h = (hg * jax.nn.sigmoid(hg)) * hu
        acc = acc + _dot(h.astype(BF), wd_ref[c0:c1, :])
    return _ln(ALPHA * x + 0.5 * acc, g, b)


def _ffn_kernel(x_ref, wg_ref, wu_ref, wd_ref, g_ref, b_ref, o_ref):
    sub = x_ref.shape[0] // FFN_SUBTILES
    for r in range(FFN_SUBTILES):
        rows = slice(r * sub, (r + 1) * sub)
        o_ref[rows, :] = _ffn_ln(x_ref[rows, :], wg_ref, wu_ref, wd_ref, g_ref[...], b_ref[...])


def _ffn_call(x2d, l, wg, wu, wd, g, b):
    n, d = x2d.shape
    tm = min(TM_ROWS * FFN_SUBTILES, n)
    return pl.pallas_call(
        _ffn_kernel,
        grid=(n // tm,),
        in_specs=[pl.BlockSpec((tm, d), lambda i: (i, 0)),
                  _layer_spec(wg, l), _layer_spec(wu, l), _layer_spec(wd, l), _layer_spec(g, l), _layer_spec(b, l)],
        out_specs=pl.BlockSpec((tm, d), lambda i: (i, 0)),
        out_shape=jax.ShapeDtypeStruct((n, d), F32),
        compiler_params=_params("parallel"),
        name="ffn_ln",
    )(x2d, wg, wu, wd, g, b)


def _rope(v, c, s1, s2, half):
    return v * c + pltpu.roll(v, LANE - half, 1) * s1 + pltpu.roll(v, half, 1) * s2


def _rms(v, g):
    return v * lax.rsqrt(jnp.mean(v * v, -1, keepdims=True) + RMS_EPS) * g


def _inproj_kernel(x_ref, wa_ref, wn_ref, wq_ref, wqr_ref, wk_ref, wv_ref, gq_ref, gkv_ref,
                   cm_ref, s1m_ref, s2m_ref, cn_ref, s1n_ref, s2n_ref,
                   qm_ref, km_ref, vm_ref, nq_ref, kc_ref, vc_ref,
                   ks_ref, vs_ref, kw_ref, vw_ref, gn_ref):
    tm = x_ref.shape[1]
    t0 = pl.program_id(1) * tm
    xb = x_ref[0].astype(BF)
    lane = lax.broadcasted_iota(jnp.int32, (tm, LANE), 1)
    row = lax.broadcasted_iota(jnp.int32, (tm, LANE), 0) + t0
    one_col = (lane == ONE_LANE).astype(F32)
    blk_onehot = (lane - ONE_LANE == (row >> LOG2_L_SLC)).astype(F32)
    cm, s1m, s2m = cm_ref[...], s1m_ref[...], s2m_ref[...]
    cn, s1n, s2n = cn_ref[...], s1n_ref[...], s2n_ref[...]
    hm = MLA_ROPE // 2
    hn = NSA_ROT // 2

    pa = _dot(xb, wa_ref[...])
    pn = _dot(xb, wn_ref[...])
    cqn = _rms(pa[:, :MLA_Q_RANK], gq_ref[...]).astype(BF)
    ckvn = _rms(pa[:, MLA_Q_RANK:MLA_Q_RANK + MLA_KV_RANK], gkv_ref[...]).astype(BF)
    kpe = _rope(pa[:, MLA_Q_RANK + MLA_KV_RANK:], cm, s1m, s2m, hm)
    q = _dot(cqn, wq_ref[...])
    q_rot = _dot(cqn, wqr_ref[...])
    kn = _dot(ckvn, wk_ref[...])
    vv = _dot(ckvn, wv_ref[...])
    c_q, s_q = cm * MLA_SCALE, (s2m - s1m) * MLA_SCALE
    for h in range(MLA_HEADS):
        sl = slice(h * LANE, (h + 1) * LANE)
        qm_ref[0, h] = (q[:, sl] * c_q + q_rot[:, sl] * s_q).astype(BF)
        km_ref[0, h] = (kn[:, sl] + kpe).astype(BF)
        vm_ref[0, h] = (vv[:, sl] + one_col).astype(BF)

    low = lane < NSA_HD

    def halves(v):
        return jnp.where(low, v, 0.0), jnp.where(low, pltpu.roll(v, NSA_HD, 1), 0.0)

    def slab(i):
        return pn[:, i * LANE:(i + 1) * LANE]

    for i in range(NSA_HEADS // 2):
        for j, qh in enumerate(halves(_rope(slab(i), cn, s1n, s2n, hn) * NSA_SCALE)):
            nq_ref[0, 2 * i + j] = qh.astype(BF)
    o = NSA_HEADS // 2
    kc_ref[0] = _rope(slab(o), cn, s1n, s2n, hn)
    vc_ref[0] = slab(o + 1)
    k_s = halves(_rope(slab(o + 2), cn, s1n, s2n, hn))
    v_s = halves(slab(o + 3))
    k_w = halves(_rope(slab(o + 4), cn, s1n, s2n, hn))
    v_w = halves(slab(o + 5))
    gn_ref[0] = jax.nn.sigmoid(slab(o + 6))
    for g in range(NSA_KV_HEADS):
        ks_ref[0, g] = (k_s[g] + blk_onehot).astype(BF)
        vs_ref[0, g] = (v_s[g] + one_col).astype(BF)
        kw_ref[0, g] = k_w[g].astype(BF)
        vw_ref[0, g] = (v_w[g] + one_col).astype(BF)


def _inproj_call(x, w, l, tabs):
    B, T, D = x.shape
    tm = min(TM_ROWS, T)
    H, G = MLA_HEADS, NSA_KV_HEADS
    head_spec = pl.BlockSpec((1, H, tm, LANE), lambda b, i: (b, 0, i, 0))
    grp_spec = pl.BlockSpec((1, G, tm, LANE), lambda b, i: (b, 0, i, 0))
    row_spec = pl.BlockSpec((1, tm, LANE), lambda b, i: (b, i, 0))
    tab_spec = pl.BlockSpec((tm, LANE), lambda b, i: (i, 0))
    head_shape = jax.ShapeDtypeStruct((B, H, T, LANE), BF)
    grp_shape = jax.ShapeDtypeStruct((B, G, T, LANE), BF)
    consts = [w["wa"], w["wn"], w["wq"], w["wqr"], w["wk"], w["wv"], w["gq"], w["gkv"]]
    return pl.pallas_call(
        _inproj_kernel,
        grid=(B, T // tm),
        in_specs=[pl.BlockSpec((1, tm, D), lambda b, i: (b, i, 0))]
                 + [_layer_spec(c, l) for c in consts] + [tab_spec] * 6,
        out_specs=[head_spec] * 4 + [row_spec] * 2 + [grp_spec] * 4 + [row_spec],
        out_shape=[head_shape] * 4 + [jax.ShapeDtypeStruct((B, T, LANE), F32)] * 2
                  + [grp_shape] * 4 + [jax.ShapeDtypeStruct((B, T, LANE), F32)],
        compiler_params=_params("parallel", "parallel"),
        name="in_proj",
    )(x, *consts, *tabs)


def _gelu_tanh(x):
    return 0.5 * x * (1.0 + jnp.tanh(np.sqrt(2.0 / np.pi) * (x + 0.044715 * (x * x * x))))


def _compress_kernel(kc_ref, vc_ref, pek_ref, pev_ref, w1k_ref, w1v_ref, b1k_ref, b1v_ref,
                     w2k_ref, w2vt_ref, kco_ref, vcto_ref):
    nc = kc_ref.shape[1]
    one_row = (lax.broadcasted_iota(jnp.int32, (LANE, nc), 0) == ONE_LANE).astype(F32)
    for src, pe, w1, b1, w2, dst, transposed in (
            (kc_ref, pek_ref, w1k_ref, b1k_ref, w2k_ref, kco_ref, False),
            (vc_ref, pev_ref, w1v_ref, b1v_ref, w2vt_ref, vcto_ref, True)):
        r = src[0]
        a_lo = (r + pe[0:1]).astype(BF)
        a_hi = (r + pe[1:2]).astype(BF)
        for g in range(NSA_KV_HEADS):
            h_hi = _dot(a_hi, w1[g, 1])
            h = _dot(a_lo, w1[g, 0]) + pltpu.roll(h_hi, nc - 1, 0) + b1[...]
            act = _gelu_tanh(h).astype(BF)
            if transposed:
                dst[0, g] = (_dot_nt(w2[...], act) + one_row).astype(BF)
            else:
                dst[0, g] = _dot(act, w2[...]).astype(BF)


def _compress_call(kc, vc, w, l):
    B, T, _ = kc.shape
    nc = T // S_CMP
    width = S_CMP * LANE
    kc2 = kc.reshape(B, nc, width)
    vc2 = vc.reshape(B, nc, width)
    consts = [w["pek"], w["pev"], w["w1k"], w["w1v"], w["b1k"], w["b1v"], w["w2k"], w["w2vt"]]
    in_spec = pl.BlockSpec((1, nc, width), lambda b: (b, 0, 0))
    return pl.pallas_call(
        _compress_kernel,
        grid=(B,),
        in_specs=[in_spec, in_spec] + [_layer_spec(c, l) for c in consts],
        out_specs=[pl.BlockSpec((1, NSA_KV_HEADS, nc, LANE), lambda b: (b, 0, 0, 0)),
                   pl.BlockSpec((1, NSA_KV_HEADS, LANE, nc), lambda b: (b, 0, 0, 0))],
        out_shape=[jax.ShapeDtypeStruct((B, NSA_KV_HEADS, nc, LANE), BF),
                   jax.ShapeDtypeStruct((B, NSA_KV_HEADS, LANE, nc), BF)],
        compiler_params=_params("parallel"),
        name="nsa_compress",
    )(kc2, vc2, *consts)


def _cmp_select_kernel(q_ref, kc_ref, vct_ref, ovt_ref, oc_ref, sel_ref, imp_ref, cnt_ref, *, n_sel):
    tq = q_ref.shape[2]
    nc = kc_ref.shape[2]
    ns = ovt_ref.shape[0]
    qs = pl.program_id(2) * tq
    kc = kc_ref[0, 0]
    vct = vct_ref[0, 0]

    t_c = lax.broadcasted_iota(jnp.int32, (nc, tq), 1) + qs
    c_c = lax.broadcasted_iota(jnp.int32, (nc, tq), 0)
    bias = jnp.where(c_c * S_CMP + (L_CMP - 1) <= t_c, 0.0, NEG)
    has_valid = jnp.where(lax.broadcasted_iota(jnp.int32, (1, tq), 1) + qs >= L_CMP - 1, 1.0, 0.0)
    bias = jnp.concatenate([bias] * NSA_HPG, axis=1)
    has_valid = jnp.concatenate([has_valid] * NSA_HPG, axis=1)
    st = _dot_nt(kc, q_ref[0].reshape(NSA_HPG * tq, LANE)) + bias
    et = jnp.exp2(st - st.max(0, keepdims=True))
    p = et * (has_valid / et.sum(0, keepdims=True))
    acc_t = _dot(vct, et.astype(BF))
    o_t = acc_t * (has_valid / acc_t[ONE_LANE:ONE_LANE + 1, :])
    p_sum = p[:, :tq]
    for h in range(1, NSA_HPG):
        p_sum = p_sum + p[:, h * tq:(h + 1) * tq]
    oc_ref[0] = jnp.concatenate([o_t[:, h * tq:(h + 1) * tq].T[:, :NSA_HD] for h in range(NSA_HPG)], axis=1)

    hi = p_sum.astype(BF)
    lo = (p_sum - hi.astype(F32)).astype(BF)
    imp = _dot(ovt_ref[...], hi) + _dot(ovt_ref[...], lo)
    j = lax.broadcasted_iota(jnp.int32, (ns, tq), 0)
    cur = (lax.broadcasted_iota(jnp.int32, (ns, tq), 1) + qs) >> LOG2_L_SLC
    forced = (j == 0) | (j == cur) | (j == cur - 1)
    imp_ref[...] = jnp.where(forced, BIG, jnp.where(j <= cur, imp, NEG))
    cnt_ref[...] = jnp.zeros((ns, tq), F32)
    cur_max = (qs + tq - 1) >> LOG2_L_SLC
    n_grp = ns // SUBLANE
    j_in = lax.broadcasted_iota(jnp.int32, (SUBLANE, tq), 0)
    for kg in range(n_grp):
        @pl.when((kg * SUBLANE <= cur_max) & (cur_max >= n_sel))
        def _():
            grps = [imp_ref[r * SUBLANE:(r + 1) * SUBLANE, :] for r in range(n_grp)]
            cnts = [cnt_ref[r * SUBLANE:(r + 1) * SUBLANE, :] for r in range(n_grp)]
            for k in range(kg * SUBLANE, (kg + 1) * SUBLANE):
                row_k = imp_ref[k:k + 1, :]
                for r, grp in enumerate(grps):
                    if r > kg:
                        ahead = jnp.where(row_k >= grp, 1.0, 0.0)
                    elif r < kg:
                        ahead = jnp.where(row_k > grp, 1.0, 0.0)
                    else:
                        ahead = jnp.where(j_in > k % SUBLANE, jnp.where(row_k >= grp, 1.0, 0.0),
                                          jnp.where(row_k > grp, 1.0, 0.0))
                    cnts[r] = cnts[r] + ahead
            for r in range(n_grp):
                cnt_ref[r * SUBLANE:(r + 1) * SUBLANE, :] = cnts[r]
    not_sel = jnp.where(cnt_ref[...] < n_sel, 0.0, 1.0)
    pieces = [jnp.zeros((ONE_LANE, tq), F32), not_sel]
    if ns < LANE - ONE_LANE:
        pieces.append(jnp.zeros((LANE - ONE_LANE - ns, tq), F32))
    sel_ref[0, 0] = (jnp.concatenate(pieces, axis=0).T * NEG).astype(BF)


def _cmp_select_call(nq, kcc, vct, ovt, n_sel):
    B, H, T, _ = nq.shape
    G = NSA_KV_HEADS
    nc = kcc.shape[2]
    ns = ovt.shape[0]
    tq = min(TQ_CMP, T)
    return pl.pallas_call(
        functools.partial(_cmp_select_kernel, n_sel=n_sel),
        grid=(B, G, T // tq),
        in_specs=[pl.BlockSpec((1, NSA_HPG, tq, LANE), lambda b, g, i: (b, g, i, 0)),
                  pl.BlockSpec((1, 1, nc, LANE), lambda b, g, i: (b, g, 0, 0)),
                  pl.BlockSpec((1, 1, LANE, nc), lambda b, g, i: (b, g, 0, 0)),
                  _const_spec(ovt.shape)],
        out_specs=[pl.BlockSpec((1, tq, NSA_HPG * NSA_HD), lambda b, g, i: (b, i, g)),
                   pl.BlockSpec((1, 1, tq, LANE), lambda b, g, i: (b, g, i, 0))],
        out_shape=[jax.ShapeDtypeStruct((B, T, H * NSA_HD), F32),
                   jax.ShapeDtypeStruct((B, G, T, LANE), BF)],
        scratch_shapes=[pltpu.VMEM((ns, tq), F32), pltpu.VMEM((ns, tq), F32)],
        compiler_params=_params("parallel", "parallel", "parallel"),
        name="nsa_cmp_select",
    )(nq, kcc, vct, ovt)


def _flash_step(q, k, v, m, acc, bias=None):
    s = _dot_nt(q, k)
    if bias is not None:
        reps = s.shape[0] // bias.shape[0]
        s = s + bias if reps == 1 else (s.reshape(reps, *bias.shape) + bias[None]).reshape(s.shape)
    m_new = jnp.maximum(m, s.max(-1, keepdims=True))
    p = jnp.exp2(s - m_new)
    acc = acc * jnp.exp2(m - m_new) + _dot(p.astype(BF), v)
    return m_new, acc


def _diag_biases(tq):
    half = tq // 2
    causal = jnp.where(lax.broadcasted_iota(jnp.int32, (half, half), 1)
                       <= lax.broadcasted_iota(jnp.int32, (half, half), 0), 0.0, NEG)
    return causal, jnp.concatenate([jnp.zeros((half, half), F32), causal], axis=1)


def _causal_tile(q_at, k_at, v_at, c, tq, causal_bias, lower_bias):
    half = tq // 2
    m, acc = jnp.full((tq, 1), NEG, F32), jnp.zeros((tq, LANE), F32)
    for j in range(c):
        m, acc = _flash_step(q_at(slice(None)), k_at(j * tq, tq), v_at(j * tq, tq), m, acc)
    off = c * tq
    _, acc_t = _flash_step(q_at(slice(0, half)), k_at(off, half), v_at(off, half), m[:half], acc[:half], causal_bias)
    _, acc_b = _flash_step(q_at(slice(half, tq)), k_at(off, tq), v_at(off, tq), m[half:], acc[half:], lower_bias)
    return jnp.concatenate([acc_t, acc_b], axis=0)


def _mla_kernel(q_ref, k_ref, v_ref, o_ref):
    nh, tq = q_ref.shape[1], q_ref.shape[2]
    biases = _diag_biases(tq)
    for c in range(k_ref.shape[2] // tq):
        @pl.when(pl.program_id(2) == c)
        def _():
            outs = []
            for hh in range(nh):
                acc = _causal_tile(lambda r: q_ref[0, hh, r, :], lambda s, n: k_ref[0, hh, s:s + n, :],
                                   lambda s, n: v_ref[0, hh, s:s + n, :], c, tq, *biases)
                outs.append(acc[:, :MLA_V] / acc[:, ONE_LANE:ONE_LANE + 1])
            o_ref[0] = jnp.concatenate(outs, axis=1).astype(o_ref.dtype)


def _mla_call(qm, km, vm):
    B, H, T, _ = qm.shape
    tq = min(TQ_MLA, T)
    nh = MLA_HEADS_PER_STEP
    kv_spec = pl.BlockSpec((1, nh, T, LANE), lambda b, h, i: (b, h, 0, 0))
    return pl.pallas_call(
        _mla_kernel,
        grid=(B, H // nh, T // tq),
        in_specs=[pl.BlockSpec((1, nh, tq, LANE), lambda b, h, i: (b, h, i, 0)), kv_spec, kv_spec],
        out_specs=pl.BlockSpec((1, tq, nh * MLA_V), lambda b, h, i: (b, i, h)),
        out_shape=jax.ShapeDtypeStruct((B, T, H * MLA_V), BF),
        compiler_params=_params("parallel", "parallel", "parallel"),
        name="mla_attention",
    )(qm, km, vm)


def _window_bias(q_start, k_start, sub, span):
    qpos = q_start + lax.broadcasted_iota(jnp.int32, (sub, 1), 0)
    kpos = k_start + lax.broadcasted_iota(jnp.int32, (sub, span), 1)
    return jnp.where(kpos <= qpos, jnp.where(kpos > qpos - WINDOW, 0.0, NEG), NEG)


def _nsa_kernel(q_ref, sel_ref, ks_ref, vs_ref, kw_ref, vw_ref, os_ref, ow_ref):
    nh, tq = q_ref.shape[1], q_ref.shape[2]
    biases = _diag_biases(tq)
    sub = min(WIN_SUB, tq)
    span = WINDOW + sub
    far_bias = _window_bias(WINDOW, 0, sub, span)

    def qa(hh, rows):
        return q_ref[0, hh, rows, :] + sel_ref[0, 0, rows, :]

    for c in range(ks_ref.shape[2] // tq):
        @pl.when(pl.program_id(2) == c)
        def _():
            o_s = []
            for hh in range(nh):
                acc = _causal_tile(lambda r: qa(hh, r), lambda s, n: ks_ref[0, 0, s:s + n, :],
                                   lambda s, n: vs_ref[0, 0, s:s + n, :], c, tq, *biases)
                o_s.append(acc[:, :NSA_HD] / acc[:, ONE_LANE:ONE_LANE + 1])
            os_ref[0] = jnp.concatenate(o_s, axis=1)

            for u in range(tq // sub):
                rows = slice(u * sub, (u + 1) * sub)
                q_start = c * tq + u * sub
                ws = max(q_start - WINDOW, 0)
                bias_w = far_bias if q_start >= WINDOW else _window_bias(q_start, ws, sub, span)
                q_u = jnp.concatenate([qa(hh, rows) for hh in range(nh)], axis=0)
                _, acc_w = _flash_step(q_u, kw_ref[0, 0, ws:ws + span, :], vw_ref[0, 0, ws:ws + span, :],
                                       jnp.full((nh * sub, 1), NEG, F32), jnp.zeros((nh * sub, LANE), F32), bias_w)
                o_u = acc_w[:, :NSA_HD] / acc_w[:, ONE_LANE:ONE_LANE + 1]
                ow_ref[0, rows, :] = jnp.concatenate([o_u[hh * sub:(hh + 1) * sub] for hh in range(nh)], axis=1)


def _nsa_call(nq, sel, ks, vs, kw, vw):
    B, H, T, _ = nq.shape
    nh = NSA_HEADS_PER_STEP
    per_grp = NSA_HPG // nh
    tq = min(TQ_NSA, T)
    sub = min(WIN_SUB, tq)
    assert WINDOW % sub == 0 and tq % sub == 0 and T >= WINDOW + sub
    kv_spec = pl.BlockSpec((1, 1, T, LANE), lambda b, h, i: (b, h // per_grp, 0, 0))
    out_spec = pl.BlockSpec((1, tq, nh * NSA_HD), lambda b, h, i: (b, i, h))
    out_shape = jax.ShapeDtypeStruct((B, T, H * NSA_HD), F32)
    return pl.pallas_call(
        _nsa_kernel,
        grid=(B, H // nh, T // tq),
        in_specs=[pl.BlockSpec((1, nh, tq, LANE), lambda b, h, i: (b, h, i, 0)),
                  pl.BlockSpec((1, 1, tq, LANE), lambda b, h, i: (b, h // per_grp, i, 0)),
                  kv_spec, kv_spec, kv_spec, kv_spec],
        out_specs=[out_spec, out_spec],
        out_shape=[out_shape, out_shape],
        compiler_params=_params("parallel", "parallel", "parallel"),
        name="nsa_select_window",
    )(nq, sel, ks, vs, kw, vw)


def _merge_kernel(x_ref, om_ref, oc_ref, os_ref, ow_ref, gb_ref, wgm_ref, wgn_ref, wpm_ref, wpn_ref, wo_ref,
                  g_ref, b_ref, o_ref):
    x = x_ref[...]
    xb = x.astype(BF)
    gates = gb_ref[...]
    tm = gates.shape[0]
    head = lax.broadcasted_iota(jnp.int32, (tm, LANE), 1) // NSA_HD
    slabs = []
    for j in range(NSA_HEADS * NSA_HD // LANE):
        sl = slice(j * LANE, (j + 1) * LANE)
        idx = 3 * (head + j * (LANE // NSA_HD))
        slabs.append(jnp.take_along_axis(gates, idx, axis=1) * oc_ref[:, sl]
                     + jnp.take_along_axis(gates, idx + 1, axis=1) * os_ref[:, sl]
                     + jnp.take_along_axis(gates, idx + 2, axis=1) * ow_ref[:, sl])
    o_n = jnp.concatenate(slabs, axis=1).astype(BF)
    y = (jax.nn.sigmoid(_dot(xb, wgm_ref[...])) * _dot(om_ref[...], wpm_ref[...])
         + jax.nn.sigmoid(_dot(xb, wgn_ref[...])) * _dot(o_n, wpn_ref[...]))
    mix = _dot(y.astype(BF), wo_ref[...])
    o_ref[...] = _ln(ALPHA * x + mix, g_ref[...], b_ref[...])


def _merge_call(x2d, om, oc, o_s, o_w, gates, w, l):
    n, d = x2d.shape
    tm = min(TM_ROWS, n)
    row = lambda a: pl.BlockSpec((tm, a.shape[1]), lambda i: (i, 0))
    consts = [w["wgm"], w["wgn"], w["wpm"], w["wpn"], w["wo"], w["ln_mix_g"], w["ln_mix_b"]]
    return pl.pallas_call(
        _merge_kernel,
        grid=(n // tm,),
        in_specs=[row(a) for a in (x2d, om, oc, o_s, o_w, gates)] + [_layer_spec(c, l) for c in consts],
        out_specs=pl.BlockSpec((tm, d), lambda i: (i, 0)),
        out_shape=jax.ShapeDtypeStruct((n, d), F32),
        compiler_params=_params("parallel"),
        name="merge_out_ln",
    )(x2d, om, oc, o_s, o_w, gates, *consts)


def _pad_last(w, width):
    return jnp.pad(w, [(0, 0)] * (w.ndim - 1) + [(0, width - w.shape[-1])])


def _stacked_weights(p):
    w_in = p["w_in"]
    nl, d = w_in.shape[:2]
    w = {}
    w["wa"] = jnp.concatenate([w_in[..., OFF_CQ:OFF_KPE], jnp.zeros((nl, d, MLA_NOPE), F32),
                               w_in[..., OFF_KPE:OFF_NQ],
                               jnp.zeros((nl, d, LANE - MLA_NOPE - MLA_ROPE), F32)], axis=-1).astype(BF)
    w["wn"] = jnp.concatenate([w_in[..., OFF_NQ:OFF_GN], _pad_last(w_in[..., OFF_GN:OFF_GM], LANE)], axis=-1).astype(BF)

    w_uq = p["w_uq"].reshape(nl, MLA_Q_RANK, MLA_HEADS, MLA_NOPE + MLA_ROPE)
    w["wq"] = _pad_last(w_uq, LANE).reshape(nl, MLA_Q_RANK, -1).astype(BF)
    pe1, pe2 = w_uq[..., MLA_NOPE:MLA_NOPE + MLA_ROPE // 2], w_uq[..., MLA_NOPE + MLA_ROPE // 2:]
    w["wqr"] = jnp.pad(jnp.concatenate([-pe2, pe1], axis=-1),
                       ((0, 0), (0, 0), (0, 0), (MLA_NOPE, LANE - MLA_NOPE - MLA_ROPE))
                       ).reshape(nl, MLA_Q_RANK, -1).astype(BF)
    w_ukv = p["w_ukv"].reshape(nl, MLA_KV_RANK, MLA_HEADS, MLA_NOPE + MLA_V)
    w["wk"] = _pad_last(w_ukv[..., :MLA_NOPE], LANE).reshape(nl, MLA_KV_RANK, -1).astype(BF)
    w["wv"] = _pad_last(w_ukv[..., MLA_NOPE:], LANE).reshape(nl, MLA_KV_RANK, -1).astype(BF)
    w["gq"] = p["q_norm_g"][:, None]
    w["gkv"] = p["kv_norm_g"][:, None]

    G = NSA_KV_HEADS
    own_group = jnp.eye(G, dtype=F32).reshape(1, G, 1, 1, G, 1, 1)
    for nm, pe, w1, b1, w2 in (("k", "cmp_pe_k", "cmp_k_w1", "cmp_k_b1", "cmp_k_w2"),
                               ("v", "cmp_pe_v", "cmp_v_w1", "cmp_v_b1", "cmp_v_w2")):
        pe_l = p[pe].reshape(nl, 2, S_CMP, 1, NSA_HD)
        w["pe" + nm] = jnp.broadcast_to(pe_l, (nl, 2, S_CMP, G, NSA_HD)).reshape(nl, 2, S_CMP * LANE)
        w1_l = p[w1].reshape(nl, 1, 2, S_CMP, 1, NSA_HD, CMP_HIDDEN)
        w["w1" + nm] = (w1_l * own_group).reshape(nl, G, 2, S_CMP * LANE, CMP_HIDDEN).astype(BF)
        w["b1" + nm] = p[b1][:, None]
        w2_l = _pad_last(p[w2], LANE).astype(BF)
        w["w2" + nm] = w2_l
        w["w2" + nm + "t"] = jnp.swapaxes(w2_l, 1, 2)

    w["wgm"] = w_in[..., OFF_GM:OFF_GNS].astype(BF)
    w["wgn"] = w_in[..., OFF_GNS:].astype(BF)
    w["wpm"] = p["w_proj_mla"].astype(BF)
    w["wpn"] = p["w_proj_nsa"].astype(BF)
    w["wo"] = p["w_out"].astype(BF)
    w["ln_mix_g"] = p["ln_mix_g"][:, None]
    w["ln_mix_b"] = p["ln_mix_b"][:, None]
    for nm in ("ffn1", "ffn2"):
        for s in ("wg", "wu", "wd"):
            w[nm + s] = p[nm + "_" + s].astype(BF)
    for nm in ("ln_f1", "ln_f2"):
        w[nm + "_g"] = p[nm + "_g"][:, None]
        w[nm + "_b"] = p[nm + "_b"][:, None]
    return w


def _rope_tables(T):
    def tables(rot_dim):
        inv = 1.0 / (ROPE_THETA ** (jnp.arange(0, rot_dim, 2, dtype=F32) / rot_dim))
        ang = jnp.arange(T, dtype=F32)[:, None] * inv[None, :]
        return jnp.cos(ang), jnp.sin(ang)

    cos_m, sin_m = tables(MLA_ROPE)
    cos_n, sin_n = tables(NSA_ROT)
    one = lambda n: jnp.ones((T, n), F32)
    zero = lambda n: jnp.zeros((T, n), F32)
    hm, hn = MLA_ROPE // 2, NSA_ROT // 2
    rest_m = LANE - MLA_NOPE - MLA_ROPE
    cm = jnp.concatenate([one(MLA_NOPE), cos_m, cos_m, one(rest_m)], 1)
    s1m = jnp.concatenate([zero(MLA_NOPE), -sin_m, zero(hm), zero(rest_m)], 1)
    s2m = jnp.concatenate([zero(MLA_NOPE), zero(hm), sin_m, zero(rest_m)], 1)
    rest_n = NSA_HD - NSA_ROT
    cn = jnp.tile(jnp.concatenate([cos_n, cos_n, one(rest_n)], 1), (1, 2))
    s1n = jnp.tile(jnp.concatenate([-sin_n, zero(hn), zero(rest_n)], 1), (1, 2))
    s2n = jnp.tile(jnp.concatenate([zero(hn), sin_n, zero(rest_n)], 1), (1, 2))
    return cm, s1m, s2m, cn, s1n, s2n


def _overlap_t(T):
    nc, ns = T // S_CMP, T // L_SLC
    n_cmp = (T - L_CMP) // S_CMP + 1
    start = np.arange(nc) * S_CMP
    blk = np.arange(ns)
    ov = ((start[None, :] < (blk[:, None] + 1) * L_SLC) & (start[None, :] + L_CMP > blk[:, None] * L_SLC)
          & (np.arange(nc)[None, :] < n_cmp))
    return jnp.asarray(ov.astype(np.float32), dtype=BF)


def kernel(x, ln_f1_g, ln_f1_b, ffn1_wg, ffn1_wu, ffn1_wd, w_in, q_norm_g, w_uq, kv_norm_g, w_ukv,
           cmp_pe_k, cmp_k_w1, cmp_k_b1, cmp_k_w2, cmp_pe_v, cmp_v_w1, cmp_v_b1, cmp_v_w2,
           w_proj_mla, w_proj_nsa, w_out, ln_mix_g, ln_mix_b,
           ffn2_wg, ffn2_wu, ffn2_wd, ln_f2_g, ln_f2_b):
    p = dict(ln_f1_g=ln_f1_g, ln_f1_b=ln_f1_b, ffn1_wg=ffn1_wg, ffn1_wu=ffn1_wu, ffn1_wd=ffn1_wd,
             w_in=w_in, q_norm_g=q_norm_g, w_uq=w_uq, kv_norm_g=kv_norm_g, w_ukv=w_ukv,
             cmp_pe_k=cmp_pe_k, cmp_k_w1=cmp_k_w1, cmp_k_b1=cmp_k_b1, cmp_k_w2=cmp_k_w2,
             cmp_pe_v=cmp_pe_v, cmp_v_w1=cmp_v_w1, cmp_v_b1=cmp_v_b1, cmp_v_w2=cmp_v_w2,
             w_proj_mla=w_proj_mla, w_proj_nsa=w_proj_nsa, w_out=w_out,
             ln_mix_g=ln_mix_g, ln_mix_b=ln_mix_b, ffn2_wg=ffn2_wg, ffn2_wu=ffn2_wu, ffn2_wd=ffn2_wd,
             ln_f2_g=ln_f2_g, ln_f2_b=ln_f2_b)
    B, T, D = x.shape
    assert T % L_SLC == 0 and T // L_SLC <= LANE - ONE_LANE
    tabs = _rope_tables(T)
    ovt = _overlap_t(T)
    n_sel = min(N_SEL, T // L_SLC)
    h = x.reshape(B * T, D)
    w = _stacked_weights(p)
    for l in range(DEPTH):
        h = _ffn_call(h, l, w["ffn1wg"], w["ffn1wu"], w["ffn1wd"], w["ln_f1_g"], w["ln_f1_b"])
        qm, km, vm, nq, kc, vc, ks, vs, kw, vw, gn = _inproj_call(h.reshape(B, T, D), w, l, tabs)
        kcc, vct = _compress_call(kc, vc, w, l)
        oc, sel = _cmp_select_call(nq, kcc, vct, ovt, n_sel)
        om = _mla_call(qm, km, vm)
        o_s, o_w = _nsa_call(nq, sel, ks, vs, kw, vw)
        flat = lambda a: a.reshape(B * T, -1)
        h = _merge_call(h, flat(om), flat(oc), flat(o_s), flat(o_w), flat(gn), w, l)
        h = _ffn_call(h, l, w["ffn2wg"], w["ffn2wu"], w["ffn2wd"], w["ln_f2_g"], w["ln_f2_b"])
    return h.reshape(B, T, D)
```

```python
import functools

import numpy as np
import jax
import jax.numpy as jnp
from jax import lax
from jax.experimental import pallas as pl
from jax.experimental.pallas import tpu as pltpu

D_MODEL = 1024
DEPTH = 2
MLA_HEADS = 8
MLA_NOPE = 64
MLA_ROPE = 32
MLA_V = 64
MLA_Q_RANK = 256
MLA_KV_RANK = 128
NSA_HEADS = 8
NSA_KV_HEADS = 2
NSA_HPG = NSA_HEADS // NSA_KV_HEADS
NSA_HD = 64
NSA_ROT = NSA_HD // 4
L_CMP = 32
S_CMP = 16
CMP_HIDDEN = 128
L_SLC = 64
LOG2_L_SLC = 6
N_SEL = 16
WINDOW = 512
ROPE_THETA = 500000.0
D_FF = 2816
ALPHA = (2 * DEPTH) ** 0.25
LN_EPS = 1e-5
RMS_EPS = 1e-6
NEG = -1e30
BIG = 1e30
LOG2E = 1.4426950408889634
MLA_SCALE = (MLA_NOPE + MLA_ROPE) ** -0.5 * LOG2E
NSA_SCALE = NSA_HD ** -0.5 * LOG2E

OFF_CQ = 0
OFF_CKV = OFF_CQ + MLA_Q_RANK
OFF_KPE = OFF_CKV + MLA_KV_RANK
OFF_NQ = OFF_KPE + MLA_ROPE
OFF_NKV = OFF_NQ + NSA_HEADS * NSA_HD
OFF_GN = OFF_NKV + 6 * NSA_KV_HEADS * NSA_HD
OFF_GM = OFF_GN + 3 * NSA_HEADS
OFF_GNS = OFF_GM + D_MODEL

LANE = 128
SUBLANE = 8
ONE_LANE = 64
VMEM_LIMIT = 56 * 1024 * 1024

BF = jnp.bfloat16
F32 = jnp.float32

MXU_DIM = 256
FF_CHUNKS = ((0, 6 * MXU_DIM), (6 * MXU_DIM, D_FF))
TM_ROWS = 512
FFN_SUBTILES = 2
TQ_MLA = 1024
MLA_HEADS_PER_STEP = 2
TQ_CMP = 1024
TQ_NSA = 1024
NSA_HEADS_PER_STEP = 2
WIN_SUB = 256


def _dot(a, b):
    return jnp.dot(a, b, preferred_element_type=F32)


def _dot_nt(a, b):
    return lax.dot_general(a, b, (((1,), (1,)), ((), ())), preferred_element_type=F32)


def _ln(y, g, b):
    mu = jnp.mean(y, -1, keepdims=True)
    d = y - mu
    var = jnp.mean(d * d, -1, keepdims=True)
    return d * lax.rsqrt(var + LN_EPS) * g + b


def _params(*sem):
    return pltpu.CompilerParams(dimension_semantics=sem, vmem_limit_bytes=VMEM_LIMIT)


def _const_spec(shape):
    n = len(shape)
    return pl.BlockSpec(shape, lambda *_: (0,) * n, pipeline_mode=pl.Buffered(1))


def _layer_spec(stacked, l):
    n = stacked.ndim - 1
    return pl.BlockSpec((None,) + stacked.shape[1:], lambda *_: (l,) + (0,) * n, pipeline_mode=pl.Buffered(1))


def _ffn_ln(x, wg_ref, wu_ref, wd_ref, g, b):
    xb = x.astype(BF)
    acc = jnp.zeros(x.shape, F32)
    for c0, c1 in FF_CHUNKS:
        hg = _dot(xb, wg_ref[:, c0:c1])
        hu = _dot(xb, wu_ref[:, c0:c1])
        h = (hg * jax.nn.sigmoid(hg)) * hu
        acc = acc + _dot(h.astype(BF), wd_ref[c0:c1, :])
    return _ln(ALPHA * x + 0.5 * acc, g, b)


def _ffn_kernel(x_ref, wg_ref, wu_ref, wd_ref, g_ref, b_ref, o_ref):
    sub = x_ref.shape[0] // FFN_SUBTILES
    for r in range(FFN_SUBTILES):
        rows = slice(r * sub, (r + 1) * sub)
        o_ref[rows, :] = _ffn_ln(x_ref[rows, :], wg_ref, wu_ref, wd_ref, g_ref[...], b_ref[...])


def _ffn_call(x2d, l, wg, wu, wd, g, b):
    n, d = x2d.shape
    tm = min(TM_ROWS * FFN_SUBTILES, n)
    return pl.pallas_call(
        _ffn_kernel,
        grid=(n // tm,),
        in_specs=[pl.BlockSpec((tm, d), lambda i: (i, 0)),
                  _layer_spec(wg, l), _layer_spec(wu, l), _layer_spec(wd, l), _layer_spec(g, l), _layer_spec(b, l)],
        out_specs=pl.BlockSpec((tm, d), lambda i: (i, 0)),
        out_shape=jax.ShapeDtypeStruct((n, d), F32),
        compiler_params=_params("parallel"),
        name="ffn_ln",
    )(x2d, wg, wu, wd, g, b)


def _rope(v, c, s1, s2, half):
    return v * c + pltpu.roll(v, LANE - half, 1) * s1 + pltpu.roll(v, half, 1) * s2


def _rms(v, g):
    return v * lax.rsqrt(jnp.mean(v * v, -1, keepdims=True) + RMS_EPS) * g


def _inproj_kernel(x_ref, wa_ref, wn_ref, wq_ref, wqr_ref, wk_ref, wv_ref, gq_ref, gkv_ref,
                   cm_ref, s1m_ref, s2m_ref, cn_ref, s1n_ref, s2n_ref,
                   qm_ref, km_ref, vm_ref, nq_ref, kc_ref, vc_ref,
                   ks_ref, vs_ref, kw_ref, vw_ref, gn_ref):
    tm = x_ref.shape[1]
    t0 = pl.program_id(1) * tm
    xb = x_ref[0].astype(BF)
    lane = lax.broadcasted_iota(jnp.int32, (tm, LANE), 1)
    row = lax.broadcasted_iota(jnp.int32, (tm, LANE), 0) + t0
    one_col = (lane == ONE_LANE).astype(F32)
    blk_onehot = (lane - ONE_LANE == (row >> LOG2_L_SLC)).astype(F32)
    cm, s1m, s2m = cm_ref[...], s1m_ref[...], s2m_ref[...]
    cn, s1n, s2n = cn_ref[...], s1n_ref[...], s2n_ref[...]
    hm = MLA_ROPE // 2
    hn = NSA_ROT // 2

    pa = _dot(xb, wa_ref[...])
    pn = _dot(xb, wn_ref[...])
    cqn = _rms(pa[:, :MLA_Q_RANK], gq_ref[...]).astype(BF)
    ckvn = _rms(pa[:, MLA_Q_RANK:MLA_Q_RANK + MLA_KV_RANK], gkv_ref[...]).astype(BF)
    kpe = _rope(pa[:, MLA_Q_RANK + MLA_KV_RANK:], cm, s1m, s2m, hm)
    q = _dot(cqn, wq_ref[...])
    q_rot = _dot(cqn, wqr_ref[...])
    kn = _dot(ckvn, wk_ref[...])
    vv = _dot(ckvn, wv_ref[...])
    c_q, s_q = cm * MLA_SCALE, (s2m - s1m) * MLA_SCALE
    for h in range(MLA_HEADS):
        sl = slice(h * LANE, (h + 1) * LANE)
        qm_ref[0, h] = (q[:, sl] * c_q + q_rot[:, sl] * s_q).astype(BF)
        km_ref[0, h] = (kn[:, sl] + kpe).astype(BF)
        vm_ref[0, h] = (vv[:, sl] + one_col).astype(BF)

    low = lane < NSA_HD

    def halves(v):
        return jnp.where(low, v, 0.0), jnp.where(low, pltpu.roll(v, NSA_HD, 1), 0.0)

    def slab(i):
        return pn[:, i * LANE:(i + 1) * LANE]

    for i in range(NSA_HEADS // 2):
        for j, qh in enumerate(halves(_rope(slab(i), cn, s1n, s2n, hn) * NSA_SCALE)):
            nq_ref[0, 2 * i + j] = qh.astype(BF)
    o = NSA_HEADS // 2
    kc_ref[0] = _rope(slab(o), cn, s1n, s2n, hn)
    vc_ref[0] = slab(o + 1)
    k_s = halves(_rope(slab(o + 2), cn, s1n, s2n, hn))
    v_s = halves(slab(o + 3))
    k_w = halves(_rope(slab(o + 4), cn, s1n, s2n, hn))
    v_w = halves(slab(o + 5))
    gn_ref[0] = jax.nn.sigmoid(slab(o + 6))
    for g in range(NSA_KV_HEADS):
        ks_ref[0, g] = (k_s[g] + blk_onehot).astype(BF)
        vs_ref[0, g] = (v_s[g] + one_col).astype(BF)
        kw_ref[0, g] = k_w[g].astype(BF)
        vw_ref[0, g] = (v_w[g] + one_col).astype(BF)


def _inproj_call(x, w, l, tabs):
    B, T, D = x.shape
    tm = min(TM_ROWS, T)
    H, G = MLA_HEADS, NSA_KV_HEADS
    head_spec = pl.BlockSpec((1, H, tm, LANE), lambda b, i: (b, 0, i, 0))
    grp_spec = pl.BlockSpec((1, G, tm, LANE), lambda b, i: (b, 0, i, 0))
    row_spec = pl.BlockSpec((1, tm, LANE), lambda b, i: (b, i, 0))
    tab_spec = pl.BlockSpec((tm, LANE), lambda b, i: (i, 0))
    head_shape = jax.ShapeDtypeStruct((B, H, T, LANE), BF)
    grp_shape = jax.ShapeDtypeStruct((B, G, T, LANE), BF)
    consts = [w["wa"], w["wn"], w["wq"], w["wqr"], w["wk"], w["wv"], w["gq"], w["gkv"]]
    return pl.pallas_call(
        _inproj_kernel,
        grid=(B, T // tm),
        in_specs=[pl.BlockSpec((1, tm, D), lambda b, i: (b, i, 0))]
                 + [_layer_spec(c, l) for c in consts] + [tab_spec] * 6,
        out_specs=[head_spec] * 4 + [row_spec] * 2 + [grp_spec] * 4 + [row_spec],
        out_shape=[head_shape] * 4 + [jax.ShapeDtypeStruct((B, T, LANE), F32)] * 2
                  + [grp_shape] * 4 + [jax.ShapeDtypeStruct((B, T, LANE), F32)],
        compiler_params=_params("parallel", "parallel"),
        name="in_proj",
    )(x, *consts, *tabs)


def _gelu_tanh(x):
    return 0.5 * x * (1.0 + jnp.tanh(np.sqrt(2.0 / np.pi) * (x + 0.044715 * (x * x * x))))


def _compress_kernel(kc_ref, vc_ref, pek_ref, pev_ref, w1k_ref, w1v_ref, b1k_ref, b1v_ref,
                     w2k_ref, w2vt_ref, kco_ref, vcto_ref):
    nc = kc_ref.shape[1]
    one_row = (lax.broadcasted_iota(jnp.int32, (LANE, nc), 0) == ONE_LANE).astype(F32)
    for src, pe, w1, b1, w2, dst, transposed in (
            (kc_ref, pek_ref, w1k_ref, b1k_ref, w2k_ref, kco_ref, False),
            (vc_ref, pev_ref, w1v_ref, b1v_ref, w2vt_ref, vcto_ref, True)):
        r = src[0]
        a_lo = (r + pe[0:1]).astype(BF)
        a_hi = (r + pe[1:2]).astype(BF)
        for g in range(NSA_KV_HEADS):
            h_hi = _dot(a_hi, w1[g, 1])
            h = _dot(a_lo, w1[g, 0]) + pltpu.roll(h_hi, nc - 1, 0) + b1[...]
            act = _gelu_tanh(h).astype(BF)
            if transposed:
                dst[0, g] = (_dot_nt(w2[...], act) + one_row).astype(BF)
            else:
                dst[0, g] = _dot(act, w2[...]).astype(BF)


def _compress_call(kc, vc, w, l):
    B, T, _ = kc.shape
    nc = T // S_CMP
    width = S_CMP * LANE
    kc2 = kc.reshape(B, nc, width)
    vc2 = vc.reshape(B, nc, width)
    consts = [w["pek"], w["pev"], w["w1k"], w["w1v"], w["b1k"], w["b1v"], w["w2k"], w["w2vt"]]
    in_spec = pl.BlockSpec((1, nc, width), lambda b: (b, 0, 0))
    return pl.pallas_call(
        _compress_kernel,
        grid=(B,),
        in_specs=[in_spec, in_spec] + [_layer_spec(c, l) for c in consts],
        out_specs=[pl.BlockSpec((1, NSA_KV_HEADS, nc, LANE), lambda b: (b, 0, 0, 0)),
                   pl.BlockSpec((1, NSA_KV_HEADS, LANE, nc), lambda b: (b, 0, 0, 0))],
        out_shape=[jax.ShapeDtypeStruct((B, NSA_KV_HEADS, nc, LANE), BF),
                   jax.ShapeDtypeStruct((B, NSA_KV_HEADS, LANE, nc), BF)],
        compiler_params=_params("parallel"),
        name="nsa_compress",
    )(kc2, vc2, *consts)


def _cmp_select_kernel(q_ref, kc_ref, vct_ref, ovt_ref, oc_ref, sel_ref, imp_ref, cnt_ref, *, n_sel):
    tq = q_ref.shape[2]
    nc = kc_ref.shape[2]
    ns = ovt_ref.shape[0]
    qs = pl.program_id(2) * tq
    kc = kc_ref[0, 0]
    vct = vct_ref[0, 0]

    t_c = lax.broadcasted_iota(jnp.int32, (nc, tq), 1) + qs
    c_c = lax.broadcasted_iota(jnp.int32, (nc, tq), 0)
    bias = jnp.where(c_c * S_CMP + (L_CMP - 1) <= t_c, 0.0, NEG)
    has_valid = jnp.where(lax.broadcasted_iota(jnp.int32, (1, tq), 1) + qs >= L_CMP - 1, 1.0, 0.0)
    bias = jnp.concatenate([bias] * NSA_HPG, axis=1)
    has_valid = jnp.concatenate([has_valid] * NSA_HPG, axis=1)
    st = _dot_nt(kc, q_ref[0].reshape(NSA_HPG * tq, LANE)) + bias
    et = jnp.exp2(st - st.max(0, keepdims=True))
    p = et * (has_valid / et.sum(0, keepdims=True))
    acc_t = _dot(vct, et.astype(BF))
    o_t = acc_t * (has_valid / acc_t[ONE_LANE:ONE_LANE + 1, :])
    p_sum = p[:, :tq]
    for h in range(1, NSA_HPG):
        p_sum = p_sum + p[:, h * tq:(h + 1) * tq]
    oc_ref[0] = jnp.concatenate([o_t[:, h * tq:(h + 1) * tq].T[:, :NSA_HD] for h in range(NSA_HPG)], axis=1)

    hi = p_sum.astype(BF)
    lo = (p_sum - hi.astype(F32)).astype(BF)
    imp = _dot(ovt_ref[...], hi) + _dot(ovt_ref[...], lo)
    j = lax.broadcasted_iota(jnp.int32, (ns, tq), 0)
    cur = (lax.broadcasted_iota(jnp.int32, (ns, tq), 1) + qs) >> LOG2_L_SLC
    forced = (j == 0) | (j == cur) | (j == cur - 1)
    imp_ref[...] = jnp.where(forced, BIG, jnp.where(j <= cur, imp, NEG))
    cnt_ref[...] = jnp.zeros((ns, tq), F32)
    cur_max = (qs + tq - 1) >> LOG2_L_SLC
    n_grp = ns // SUBLANE
    j_in = lax.broadcasted_iota(jnp.int32, (SUBLANE, tq), 0)
    for kg in range(n_grp):
        @pl.when((kg * SUBLANE <= cur_max) & (cur_max >= n_sel))
        def _():
            grps = [imp_ref[r * SUBLANE:(r + 1) * SUBLANE, :] for r in range(n_grp)]
            cnts = [cnt_ref[r * SUBLANE:(r + 1) * SUBLANE, :] for r in range(n_grp)]
            for k in range(kg * SUBLANE, (kg + 1) * SUBLANE):
                row_k = imp_ref[k:k + 1, :]
                for r, grp in enumerate(grps):
                    if r > kg:
                        ahead = jnp.where(row_k >= grp, 1.0, 0.0)
                    elif r < kg:
                        ahead = jnp.where(row_k > grp, 1.0, 0.0)
                    else:
                        ahead = jnp.where(j_in > k % SUBLANE, jnp.where(row_k >= grp, 1.0, 0.0),
                                          jnp.where(row_k > grp, 1.0, 0.0))
                    cnts[r] = cnts[r] + ahead
            for r in range(n_grp):
                cnt_ref[r * SUBLANE:(r + 1) * SUBLANE, :] = cnts[r]
    not_sel = jnp.where(cnt_ref[...] < n_sel, 0.0, 1.0)
    pieces = [jnp.zeros((ONE_LANE, tq), F32), not_sel]
    if ns < LANE - ONE_LANE:
        pieces.append(jnp.zeros((LANE - ONE_LANE - ns, tq), F32))
    sel_ref[0, 0] = (jnp.concatenate(pieces, axis=0).T * NEG).astype(BF)


def _cmp_select_call(nq, kcc, vct, ovt, n_sel):
    B, H, T, _ = nq.shape
    G = NSA_KV_HEADS
    nc = kcc.shape[2]
    ns = ovt.shape[0]
    tq = min(TQ_CMP, T)
    return pl.pallas_call(
        functools.partial(_cmp_select_kernel, n_sel=n_sel),
        grid=(B, G, T // tq),
        in_specs=[pl.BlockSpec((1, NSA_HPG, tq, LANE), lambda b, g, i: (b, g, i, 0)),
                  pl.BlockSpec((1, 1, nc, LANE), lambda b, g, i: (b, g, 0, 0)),
                  pl.BlockSpec((1, 1, LANE, nc), lambda b, g, i: (b, g, 0, 0)),
                  _const_spec(ovt.shape)],
        out_specs=[pl.BlockSpec((1, tq, NSA_HPG * NSA_HD), lambda b, g, i: (b, i, g)),
                   pl.BlockSpec((1, 1, tq, LANE), lambda b, g, i: (b, g, i, 0))],
        out_shape=[jax.ShapeDtypeStruct((B, T, H * NSA_HD), F32),
                   jax.ShapeDtypeStruct((B, G, T, LANE), BF)],
        scratch_shapes=[pltpu.VMEM((ns, tq), F32), pltpu.VMEM((ns, tq), F32)],
        compiler_params=_params("parallel", "parallel", "parallel"),
        name="nsa_cmp_select",
    )(nq, kcc, vct, ovt)


def _flash_step(q, k, v, m, acc, bias=None):
    s = _dot_nt(q, k)
    if bias is not None:
        reps = s.shape[0] // bias.shape[0]
        s = s + bias if reps == 1 else (s.reshape(reps, *bias.shape) + bias[None]).reshape(s.shape)
    m_new = jnp.maximum(m, s.max(-1, keepdims=True))
    p = jnp.exp2(s - m_new)
    acc = acc * jnp.exp2(m - m_new) + _dot(p.astype(BF), v)
    return m_new, acc


def _diag_biases(tq):
    half = tq // 2
    causal = jnp.where(lax.broadcasted_iota(jnp.int32, (half, half), 1)
                       <= lax.broadcasted_iota(jnp.int32, (half, half), 0), 0.0, NEG)
    return causal, jnp.concatenate([jnp.zeros((half, half), F32), causal], axis=1)


def _causal_tile(q_at, k_at, v_at, c, tq, causal_bias, lower_bias):
    half = tq // 2
    m, acc = jnp.full((tq, 1), NEG, F32), jnp.zeros((tq, LANE), F32)
    for j in range(c):
        m, acc = _flash_step(q_at(slice(None)), k_at(j * tq, tq), v_at(j * tq, tq), m, acc)
    off = c * tq
    _, acc_t = _flash_step(q_at(slice(0, half)), k_at(off, half), v_at(off, half), m[:half], acc[:half], causal_bias)
    _, acc_b = _flash_step(q_at(slice(half, tq)), k_at(off, tq), v_at(off, tq), m[half:], acc[half:], lower_bias)
    return jnp.concatenate([acc_t, acc_b], axis=0)


def _mla_kernel(q_ref, k_ref, v_ref, o_ref):
    nh, tq = q_ref.shape[1], q_ref.shape[2]
    biases = _diag_biases(tq)
    for c in range(k_ref.shape[2] // tq):
        @pl.when(pl.program_id(2) == c)
        def _():
            outs = []
            for hh in range(nh):
                acc = _causal_tile(lambda r: q_ref[0, hh, r, :], lambda s, n: k_ref[0, hh, s:s + n, :],
                                   lambda s, n: v_ref[0, hh, s:s + n, :], c, tq, *biases)
                outs.append(acc[:, :MLA_V] / acc[:, ONE_LANE:ONE_LANE + 1])
            o_ref[0] = jnp.concatenate(outs, axis=1).astype(o_ref.dtype)


def _mla_call(qm, km, vm):
    B, H, T, _ = qm.shape
    tq = min(TQ_MLA, T)
    nh = MLA_HEADS_PER_STEP
    kv_spec = pl.BlockSpec((1, nh, T, LANE), lambda b, h, i: (b, h, 0, 0))
    return pl.pallas_call(
        _mla_kernel,
        grid=(B, H // nh, T // tq),
        in_specs=[pl.BlockSpec((1, nh, tq, LANE), lambda b, h, i: (b, h, i, 0)), kv_spec, kv_spec],
        out_specs=pl.BlockSpec((1, tq, nh * MLA_V), lambda b, h, i: (b, i, h)),
        out_shape=jax.ShapeDtypeStruct((B, T, H * MLA_V), BF),
        compiler_params=_params("parallel", "parallel", "parallel"),
        name="mla_attention",
    )(qm, km, vm)


def _window_bias(q_start, k_start, sub, span):
    qpos = q_start + lax.broadcasted_iota(jnp.int32, (sub, 1), 0)
    kpos = k_start + lax.broadcasted_iota(jnp.int32, (sub, span), 1)
    return jnp.where(kpos <= qpos, jnp.where(kpos > qpos - WINDOW, 0.0, NEG), NEG)


def _nsa_kernel(q_ref, sel_ref, ks_ref, vs_ref, kw_ref, vw_ref, os_ref, ow_ref):
    nh, tq = q_ref.shape[1], q_ref.shape[2]
    biases = _diag_biases(tq)
    sub = min(WIN_SUB, tq)
    span = WINDOW + sub
    far_bias = _window_bias(WINDOW, 0, sub, span)

    def qa(hh, rows):
        return q_ref[0, hh, rows, :] + sel_ref[0, 0, rows, :]

    for c in range(ks_ref.shape[2] // tq):
        @pl.when(pl.program_id(2) == c)
        def _():
            o_s = []
            for hh in range(nh):
                acc = _causal_tile(lambda r: qa(hh, r), lambda s, n: ks_ref[0, 0, s:s + n, :],
                                   lambda s, n: vs_ref[0, 0, s:s + n, :], c, tq, *biases)
                o_s.append(acc[:, :NSA_HD] / acc[:, ONE_LANE:ONE_LANE + 1])
            os_ref[0] = jnp.concatenate(o_s, axis=1)

            for u in range(tq // sub):
                rows = slice(u * sub, (u + 1) * sub)
                q_start = c * tq + u * sub
                ws = max(q_start - WINDOW, 0)
                bias_w = far_bias if q_start >= WINDOW else _window_bias(q_start, ws, sub, span)
                q_u = jnp.concatenate([qa(hh, rows) for hh in range(nh)], axis=0)
                _, acc_w = _flash_step(q_u, kw_ref[0, 0, ws:ws + span, :], vw_ref[0, 0, ws:ws + span, :],
                                       jnp.full((nh * sub, 1), NEG, F32), jnp.zeros((nh * sub, LANE), F32), bias_w)
                o_u = acc_w[:, :NSA_HD] / acc_w[:, ONE_LANE:ONE_LANE + 1]
                ow_ref[0, rows, :] = jnp.concatenate([o_u[hh * sub:(hh + 1) * sub] for hh in range(nh)], axis=1)


def _nsa_call(nq, sel, ks, vs, kw, vw):
    B, H, T, _ = nq.shape
    nh = NSA_HEADS_PER_STEP
    per_grp = NSA_HPG // nh
    tq = min(TQ_NSA, T)
    sub = min(WIN_SUB, tq)
    assert WINDOW % sub == 0 and tq % sub == 0 and T >= WINDOW + sub
    kv_spec = pl.BlockSpec((1, 1, T, LANE), lambda b, h, i: (b, h // per_grp, 0, 0))
    out_spec = pl.BlockSpec((1, tq, nh * NSA_HD), lambda b, h, i: (b, i, h))
    out_shape = jax.ShapeDtypeStruct((B, T, H * NSA_HD), F32)
    return pl.pallas_call(
        _nsa_kernel,
        grid=(B, H // nh, T // tq),
        in_specs=[pl.BlockSpec((1, nh, tq, LANE), lambda b, h, i: (b, h, i, 0)),
                  pl.BlockSpec((1, 1, tq, LANE), lambda b, h, i: (b, h // per_grp, i, 0)),
                  kv_spec, kv_spec, kv_spec, kv_spec],
        out_specs=[out_spec, out_spec],
        out_shape=[out_shape, out_shape],
        compiler_params=_params("parallel", "parallel", "parallel"),
        name="nsa_select_window",
    )(nq, sel, ks, vs, kw, vw)


def _merge_kernel(x_ref, om_ref, oc_ref, os_ref, ow_ref, gb_ref, wgm_ref, wgn_ref, wpm_ref, wpn_ref, wo_ref,
                  g_ref, b_ref, o_ref):
    x = x_ref[...]
    xb = x.astype(BF)
    gates = gb_ref[...]
    tm = gates.shape[0]
    head = lax.broadcasted_iota(jnp.int32, (tm, LANE), 1) // NSA_HD
    slabs = []
    for j in range(NSA_HEADS * NSA_HD // LANE):
        sl = slice(j * LANE, (j + 1) * LANE)
        idx = 3 * (head + j * (LANE // NSA_HD))
        slabs.append(jnp.take_along_axis(gates, idx, axis=1) * oc_ref[:, sl]
                     + jnp.take_along_axis(gates, idx + 1, axis=1) * os_ref[:, sl]
                     + jnp.take_along_axis(gates, idx + 2, axis=1) * ow_ref[:, sl])
    o_n = jnp.concatenate(slabs, axis=1).astype(BF)
    y = (jax.nn.sigmoid(_dot(xb, wgm_ref[...])) * _dot(om_ref[...], wpm_ref[...])
         + jax.nn.sigmoid(_dot(xb, wgn_ref[...])) * _dot(o_n, wpn_ref[...]))
    mix = _dot(y.astype(BF), wo_ref[...])
    o_ref[...] = _ln(ALPHA * x + mix, g_ref[...], b_ref[...])


def _merge_call(x2d, om, oc, o_s, o_w, gates, w, l):
    n, d = x2d.shape
    tm = min(TM_ROWS, n)
    row = lambda a: pl.BlockSpec((tm, a.shape[1]), lambda i: (i, 0))
    consts = [w["wgm"], w["wgn"], w["wpm"], w["wpn"], w["wo"], w["ln_mix_g"], w["ln_mix_b"]]
    return pl.pallas_call(
        _merge_kernel,
        grid=(n // tm,),
        in_specs=[row(a) for a in (x2d, om, oc, o_s, o_w, gates)] + [_layer_spec(c, l) for c in consts],
        out_specs=pl.BlockSpec((tm, d), lambda i: (i, 0)),
        out_shape=jax.ShapeDtypeStruct((n, d), F32),
        compiler_params=_params("parallel"),
        name="merge_out_ln",
    )(x2d, om, oc, o_s, o_w, gates, *consts)


def _pad_last(w, width):
    return jnp.pad(w, [(0, 0)] * (w.ndim - 1) + [(0, width - w.shape[-1])])


def _stacked_weights(p):
    w_in = p["w_in"]
    nl, d = w_in.shape[:2]
    w = {}
    w["wa"] = jnp.concatenate([w_in[..., OFF_CQ:OFF_KPE], jnp.zeros((nl, d, MLA_NOPE), F32),
                               w_in[..., OFF_KPE:OFF_NQ],
                               jnp.zeros((nl, d, LANE - MLA_NOPE - MLA_ROPE), F32)], axis=-1).astype(BF)
    w["wn"] = jnp.concatenate([w_in[..., OFF_NQ:OFF_GN], _pad_last(w_in[..., OFF_GN:OFF_GM], LANE)], axis=-1).astype(BF)

    w_uq = p["w_uq"].reshape(nl, MLA_Q_RANK, MLA_HEADS, MLA_NOPE + MLA_ROPE)
    w["wq"] = _pad_last(w_uq, LANE).reshape(nl, MLA_Q_RANK, -1).astype(BF)
    pe1, pe2 = w_uq[..., MLA_NOPE:MLA_NOPE + MLA_ROPE // 2], w_uq[..., MLA_NOPE + MLA_ROPE // 2:]
    w["wqr"] = jnp.pad(jnp.concatenate([-pe2, pe1], axis=-1),
                       ((0, 0), (0, 0), (0, 0), (MLA_NOPE, LANE - MLA_NOPE - MLA_ROPE))
                       ).reshape(nl, MLA_Q_RANK, -1).astype(BF)
    w_ukv = p["w_ukv"].reshape(nl, MLA_KV_RANK, MLA_HEADS, MLA_NOPE + MLA_V)
    w["wk"] = _pad_last(w_ukv[..., :MLA_NOPE], LANE).reshape(nl, MLA_KV_RANK, -1).astype(BF)
    w["wv"] = _pad_last(w_ukv[..., MLA_NOPE:], LANE).reshape(nl, MLA_KV_RANK, -1).astype(BF)
    w["gq"] = p["q_norm_g"][:, None]
    w["gkv"] = p["kv_norm_g"][:, None]

    G = NSA_KV_HEADS
    own_group = jnp.eye(G, dtype=F32).reshape(1, G, 1, 1, G, 1, 1)
    for nm, pe, w1, b1, w2 in (("k", "cmp_pe_k", "cmp_k_w1", "cmp_k_b1", "cmp_k_w2"),
                               ("v", "cmp_pe_v", "cmp_v_w1", "cmp_v_b1", "cmp_v_w2")):
        pe_l = p[pe].reshape(nl, 2, S_CMP, 1, NSA_HD)
        w["pe" + nm] = jnp.broadcast_to(pe_l, (nl, 2, S_CMP, G, NSA_HD)).reshape(nl, 2, S_CMP * LANE)
        w1_l = p[w1].reshape(nl, 1, 2, S_CMP, 1, NSA_HD, CMP_HIDDEN)
        w["w1" + nm] = (w1_l * own_group).reshape(nl, G, 2, S_CMP * LANE, CMP_HIDDEN).astype(BF)
        w["b1" + nm] = p[b1][:, None]
        w2_l = _pad_last(p[w2], LANE).astype(BF)
        w["w2" + nm] = w2_l
        w["w2" + nm + "t"] = jnp.swapaxes(w2_l, 1, 2)

    w["wgm"] = w_in[..., OFF_GM:OFF_GNS].astype(BF)
    w["wgn"] = w_in[..., OFF_GNS:].astype(BF)
    w["wpm"] = p["w_proj_mla"].astype(BF)
    w["wpn"] = p["w_proj_nsa"].astype(BF)
    w["wo"] = p["w_out"].astype(BF)
    w["ln_mix_g"] = p["ln_mix_g"][:, None]
    w["ln_mix_b"] = p["ln_mix_b"][:, None]
    for nm in ("ffn1", "ffn2"):
        for s in ("wg", "wu", "wd"):
            w[nm + s] = p[nm + "_" + s].astype(BF)
    for nm in ("ln_f1", "ln_f2"):
        w[nm + "_g"] = p[nm + "_g"][:, None]
        w[nm + "_b"] = p[nm + "_b"][:, None]
    return w


def _rope_tables(T):
    def tables(rot_dim):
        inv = 1.0 / (ROPE_THETA ** (jnp.arange(0, rot_dim, 2, dtype=F32) / rot_dim))
        ang = jnp.arange(T, dtype=F32)[:, None] * inv[None, :]
        return jnp.cos(ang), jnp.sin(ang)

    cos_m, sin_m = tables(MLA_ROPE)
    cos_n, sin_n = tables(NSA_ROT)
    one = lambda n: jnp.ones((T, n), F32)
    zero = lambda n: jnp.zeros((T, n), F32)
    hm, hn = MLA_ROPE // 2, NSA_ROT // 2
    rest_m = LANE - MLA_NOPE - MLA_ROPE
    cm = jnp.concatenate([one(MLA_NOPE), cos_m, cos_m, one(rest_m)], 1)
    s1m = jnp.concatenate([zero(MLA_NOPE), -sin_m, zero(hm), zero(rest_m)], 1)
    s2m = jnp.concatenate([zero(MLA_NOPE), zero(hm), sin_m, zero(rest_m)], 1)
    rest_n = NSA_HD - NSA_ROT
    cn = jnp.tile(jnp.concatenate([cos_n, cos_n, one(rest_n)], 1), (1, 2))
    s1n = jnp.tile(jnp.concatenate([-sin_n, zero(hn), zero(rest_n)], 1), (1, 2))
    s2n = jnp.tile(jnp.concatenate([zero(hn), sin_n, zero(rest_n)], 1), (1, 2))
    return cm, s1m, s2m, cn, s1n, s2n


def _overlap_t(T):
    nc, ns = T // S_CMP, T // L_SLC
    n_cmp = (T - L_CMP) // S_CMP + 1
    start = np.arange(nc) * S_CMP
    blk = np.arange(ns)
    ov = ((start[None, :] < (blk[:, None] + 1) * L_SLC) & (start[None, :] + L_CMP > blk[:, None] * L_SLC)
          & (np.arange(nc)[None, :] < n_cmp))
    return jnp.asarray(ov.astype(np.float32), dtype=BF)


def kernel(x, ln_f1_g, ln_f1_b, ffn1_wg, ffn1_wu, ffn1_wd, w_in, q_norm_g, w_uq, kv_norm_g, w_ukv,
           cmp_pe_k, cmp_k_w1, cmp_k_b1, cmp_k_w2, cmp_pe_v, cmp_v_w1, cmp_v_b1, cmp_v_w2,
           w_proj_mla, w_proj_nsa, w_out, ln_mix_g, ln_mix_b,
           ffn2_wg, ffn2_wu, ffn2_wd, ln_f2_g, ln_f2_b):
    p = dict(ln_f1_g=ln_f1_g, ln_f1_b=ln_f1_b, ffn1_wg=ffn1_wg, ffn1_wu=ffn1_wu, ffn1_wd=ffn1_wd,
             w_in=w_in, q_norm_g=q_norm_g, w_uq=w_uq, kv_norm_g=kv_norm_g, w_ukv=w_ukv,
             cmp_pe_k=cmp_pe_k, cmp_k_w1=cmp_k_w1, cmp_k_b1=cmp_k_b1, cmp_k_w2=cmp_k_w2,
             cmp_pe_v=cmp_pe_v, cmp_v_w1=cmp_v_w1, cmp_v_b1=cmp_v_b1, cmp_v_w2=cmp_v_w2,
             w_proj_mla=w_proj_mla, w_proj_nsa=w_proj_nsa, w_out=w_out,
             ln_mix_g=ln_mix_g, ln_mix_b=ln_mix_b, ffn2_wg=ffn2_wg, ffn2_wu=ffn2_wu, ffn2_wd=ffn2_wd,
             ln_f2_g=ln_f2_g, ln_f2_b=ln_f2_b)
    B, T, D = x.shape
    assert T % L_SLC == 0 and T // L_SLC <= LANE - ONE_LANE
    tabs = _rope_tables(T)
    ovt = _overlap_t(T)
    n_sel = min(N_SEL, T // L_SLC)
    h = x.reshape(B * T, D)
    w = _stacked_weights(p)
    for l in range(DEPTH):
        h = _ffn_call(h, l, w["ffn1wg"], w["ffn1wu"], w["ffn1wd"], w["ln_f1_g"], w["ln_f1_b"])
        qm, km, vm, nq, kc, vc, ks, vs, kw, vw, gn = _inproj_call(h.reshape(B, T, D), w, l, tabs)
        kcc, vct = _compress_call(kc, vc, w, l)
        oc, sel = _cmp_select_call(nq, kcc, vct, ovt, n_sel)
        om = _mla_call(qm, km, vm)
        o_s, o_w = _nsa_call(nq, sel, ks, vs, kw, vw)
        flat = lambda a: a.reshape(B * T, -1)
        h = _merge_call(h, flat(om), flat(oc), flat(o_s), flat(o_w), flat(gn), w, l)
        h = _ffn_call(h, l, w["ffn2wg"], w["ffn2wu"], w["ffn2wd"], w["ln_f2_g"], w["ln_f2_b"])
    return h.reshape(B, T, D)
```

```python
import functools

import numpy as np
import jax
import jax.numpy as jnp
from jax import lax
from jax.experimental import pallas as pl
from jax.experimental.pallas import tpu as pltpu

D_MODEL = 1024
DEPTH = 2
MLA_HEADS = 8
MLA_NOPE = 64
MLA_ROPE = 32
MLA_V = 64
MLA_Q_RANK = 256
MLA_KV_RANK = 128
NSA_HEADS = 8
NSA_KV_HEADS = 2
NSA_HPG = NSA_HEADS // NSA_KV_HEADS
NSA_HD = 64
NSA_ROT = NSA_HD // 4
L_CMP = 32
S_CMP = 16
CMP_HIDDEN = 128
L_SLC = 64
LOG2_L_SLC = 6
N_SEL = 16
WINDOW = 512
ROPE_THETA = 500000.0
D_FF = 2816
ALPHA = (2 * DEPTH) ** 0.25
LN_EPS = 1e-5
RMS_EPS = 1e-6
NEG = -1e30
BIG = 1e30
LOG2E = 1.4426950408889634
MLA_SCALE = (MLA_NOPE + MLA_ROPE) ** -0.5 * LOG2E
NSA_SCALE = NSA_HD ** -0.5 * LOG2E

OFF_CQ = 0
OFF_CKV = OFF_CQ + MLA_Q_RANK
OFF_KPE = OFF_CKV + MLA_KV_RANK
OFF_NQ = OFF_KPE + MLA_ROPE
OFF_NKV = OFF_NQ + NSA_HEADS * NSA_HD
OFF_GN = OFF_NKV + 6 * NSA_KV_HEADS * NSA_HD
OFF_GM = OFF_GN + 3 * NSA_HEADS
OFF_GNS = OFF_GM + D_MODEL

LANE = 128
SUBLANE = 8
ONE_LANE = 64
VMEM_LIMIT = 56 * 1024 * 1024

BF = jnp.bfloat16
F32 = jnp.float32

MXU_DIM = 256
FF_CHUNKS = ((0, 6 * MXU_DIM), (6 * MXU_DIM, D_FF))
TM_ROWS = 512
FFN_SUBTILES = 2
TQ_MLA = 1024
MLA_HEADS_PER_STEP = 2
TQ_CMP = 1024
TQ_NSA = 1024
NSA_HEADS_PER_STEP = 2
WIN_SUB = 256


def _dot(a, b):
    return jnp.dot(a, b, preferred_element_type=F32)


def _dot_nt(a, b):
    return lax.dot_general(a, b, (((1,), (1,)), ((), ())), preferred_element_type=F32)


def _ln(y, g, b):
    mu = jnp.mean(y, -1, keepdims=True)
    d = y - mu
    var = jnp.mean(d * d, -1, keepdims=True)
    return d * lax.rsqrt(var + LN_EPS) * g + b


def _params(*sem):
    return pltpu.CompilerParams(dimension_semantics=sem, vmem_limit_bytes=VMEM_LIMIT)


def _const_spec(shape):
    n = len(shape)
    return pl.BlockSpec(shape, lambda *_: (0,) * n, pipeline_mode=pl.Buffered(1))


def _layer_spec(stacked, l):
    n = stacked.ndim - 1
    return pl.BlockSpec((None,) + stacked.shape[1:], lambda *_: (l,) + (0,) * n, pipeline_mode=pl.Buffered(1))


def _ffn_ln(x, wg_ref, wu_ref, wd_ref, g, b):
    xb = x.astype(BF)
    acc = jnp.zeros(x.shape, F32)
    for c0, c1 in FF_CHUNKS:
        hg = _dot(xb, wg_ref[:, c0:c1])
        hu = _dot(xb, wu_ref[:, c0:c1])
        h = (hg * jax.nn.sigmoid(hg)) * hu
        acc = acc + _dot(h.astype(BF), wd_ref[c0:c1, :])
    return _ln(ALPHA * x + 0.5 * acc, g, b)


def _ffn_kernel(x_ref, wg_ref, wu_ref, wd_ref, g_ref, b_ref, o_ref):
    sub = x_ref.shape[0] // FFN_SUBTILES
    for r in range(FFN_SUBTILES):
        rows = slice(r * sub, (r + 1) * sub)
        o_ref[rows, :] = _ffn_ln(x_ref[rows, :], wg_ref, wu_ref, wd_ref, g_ref[...], b_ref[...])


def _ffn_call(x2d, l, wg, wu, wd, g, b):
    n, d = x2d.shape
    tm = min(TM_ROWS * FFN_SUBTILES, n)
    return pl.pallas_call(
        _ffn_kernel,
        grid=(n // tm,),
        in_specs=[pl.BlockSpec((tm, d), lambda i: (i, 0)),
                  _layer_spec(wg, l), _layer_spec(wu, l), _layer_spec(wd, l), _layer_spec(g, l), _layer_spec(b, l)],
        out_specs=pl.BlockSpec((tm, d), lambda i: (i, 0)),
        out_shape=jax.ShapeDtypeStruct((n, d), F32),
        compiler_params=_params("parallel"),
        name="ffn_ln",
    )(x2d, wg, wu, wd, g, b)


def _rope(v, c, s1, s2, half):
    return v * c + pltpu.roll(v, LANE - half, 1) * s1 + pltpu.roll(v, half, 1) * s2


def _rms(v, g):
    return v * lax.rsqrt(jnp.mean(v * v, -1, keepdims=True) + RMS_EPS) * g


def _inproj_kernel(x_ref, wa_ref, wn_ref, wq_ref, wqr_ref, wk_ref, wv_ref, gq_ref, gkv_ref,
                   cm_ref, s1m_ref, s2m_ref, cn_ref, s1n_ref, s2n_ref,
                   qm_ref, km_ref, vm_ref, nq_ref, kc_ref, vc_ref,
                   ks_ref, vs_ref, kw_ref, vw_ref, gn_ref):
    tm = x_ref.shape[1]
    t0 = pl.program_id(1) * tm
    xb = x_ref[0].astype(BF)
    lane = lax.broadcasted_iota(jnp.int32, (tm, LANE), 1)
    row = lax.broadcasted_iota(jnp.int32, (tm, LANE), 0) + t0
    one_col = (lane == ONE_LANE).astype(F32)
    blk_onehot = (lane - ONE_LANE == (row >> LOG2_L_SLC)).astype(F32)
    cm, s1m, s2m = cm_ref[...], s1m_ref[...], s2m_ref[...]
    cn, s1n, s2n = cn_ref[...], s1n_ref[...], s2n_ref[...]
    hm = MLA_ROPE // 2
    hn = NSA_ROT // 2

    pa = _dot(xb, wa_ref[...])
    pn = _dot(xb, wn_ref[...])
    cqn = _rms(pa[:, :MLA_Q_RANK], gq_ref[...]).astype(BF)
    ckvn = _rms(pa[:, MLA_Q_RANK:MLA_Q_RANK + MLA_KV_RANK], gkv_ref[...]).astype(BF)
    kpe = _rope(pa[:, MLA_Q_RANK + MLA_KV_RANK:], cm, s1m, s2m, hm)
    q = _dot(cqn, wq_ref[...])
    q_rot = _dot(cqn, wqr_ref[...])
    kn = _dot(ckvn, wk_ref[...])
    vv = _dot(ckvn, wv_ref[...])
    c_q, s_q = cm * MLA_SCALE, (s2m - s1m) * MLA_SCALE
    for h in range(MLA_HEADS):
        sl = slice(h * LANE, (h + 1) * LANE)
        qm_ref[0, h] = (q[:, sl] * c_q + q_rot[:, sl] * s_q).astype(BF)
        km_ref[0, h] = (kn[:, sl] + kpe).astype(BF)
        vm_ref[0, h] = (vv[:, sl] + one_col).astype(BF)

    low = lane < NSA_HD

    def halves(v):
        return jnp.where(low, v, 0.0), jnp.where(low, pltpu.roll(v, NSA_HD, 1), 0.0)

    def slab(i):
        return pn[:, i * LANE:(i + 1) * LANE]

    for i in range(NSA_HEADS // 2):
        for j, qh in enumerate(halves(_rope(slab(i), cn, s1n, s2n, hn) * NSA_SCALE)):
            nq_ref[0, 2 * i + j] = qh.astype(BF)
    o = NSA_HEADS // 2
    kc_ref[0] = _rope(slab(o), cn, s1n, s2n, hn)
    vc_ref[0] = slab(o + 1)
    k_s = halves(_rope(slab(o + 2), cn, s1n, s2n, hn))
    v_s = halves(slab(o + 3))
    k_w = halves(_rope(slab(o + 4), cn, s1n, s2n, hn))
    v_w = halves(slab(o + 5))
    gn_ref[0] = jax.nn.sigmoid(slab(o + 6))
    for g in range(NSA_KV_HEADS):
        ks_ref[0, g] = (k_s[g] + blk_onehot).astype(BF)
        vs_ref[0, g] = (v_s[g] + one_col).astype(BF)
        kw_ref[0, g] = k_w[g].astype(BF)
        vw_ref[0, g] = (v_w[g] + one_col).astype(BF)


def _inproj_call(x, w, l, tabs):
    B, T, D = x.shape
    tm = min(TM_ROWS, T)
    H, G = MLA_HEADS, NSA_KV_HEADS
    head_spec = pl.BlockSpec((1, H, tm, LANE), lambda b, i: (b, 0, i, 0))
    grp_spec = pl.BlockSpec((1, G, tm, LANE), lambda b, i: (b, 0, i, 0))
    row_spec = pl.BlockSpec((1, tm, LANE), lambda b, i: (b, i, 0))
    tab_spec = pl.BlockSpec((tm, LANE), lambda b, i: (i, 0))
    head_shape = jax.ShapeDtypeStruct((B, H, T, LANE), BF)
    grp_shape = jax.ShapeDtypeStruct((B, G, T, LANE), BF)
    consts = [w["wa"], w["wn"], w["wq"], w["wqr"], w["wk"], w["wv"], w["gq"], w["gkv"]]
    return pl.pallas_call(
        _inproj_kernel,
        grid=(B, T // tm),
        in_specs=[pl.BlockSpec((1, tm, D), lambda b, i: (b, i, 0))]
                 + [_layer_spec(c, l) for c in consts] + [tab_spec] * 6,
        out_specs=[head_spec] * 4 + [row_spec] * 2 + [grp_spec] * 4 + [row_spec],
        out_shape=[head_shape] * 4 + [jax.ShapeDtypeStruct((B, T, LANE), F32)] * 2
                  + [grp_shape] * 4 + [jax.ShapeDtypeStruct((B, T, LANE), F32)],
        compiler_params=_params("parallel", "parallel"),
        name="in_proj",
    )(x, *consts, *tabs)


def _gelu_tanh(x):
    return 0.5 * x * (1.0 + jnp.tanh(np.sqrt(2.0 / np.pi) * (x + 0.044715 * (x * x * x))))


def _compress_kernel(kc_ref, vc_ref, pek_ref, pev_ref, w1k_ref, w1v_ref, b1k_ref, b1v_ref,
                     w2k_ref, w2vt_ref, kco_ref, vcto_ref):
    nc = kc_ref.shape[1]
    one_row = (lax.broadcasted_iota(jnp.int32, (LANE, nc), 0) == ONE_LANE).astype(F32)
    for src, pe, w1, b1, w2, dst, transposed in (
            (kc_ref, pek_ref, w1k_ref, b1k_ref, w2k_ref, kco_ref, False),
            (vc_ref, pev_ref, w1v_ref, b1v_ref, w2vt_ref, vcto_ref, True)):
        r = src[0]
        a_lo = (r + pe[0:1]).astype(BF)
        a_hi = (r + pe[1:2]).astype(BF)
        for g in range(NSA_KV_HEADS):
            h_hi = _dot(a_hi, w1[g, 1])
            h = _dot(a_lo, w1[g, 0]) + pltpu.roll(h_hi, nc - 1, 0) + b1[...]
            act = _gelu_tanh(h).astype(BF)
            if transposed:
                dst[0, g] = (_dot_nt(w2[...], act) + one_row).astype(BF)
            else:
                dst[0, g] = _dot(act, w2[...]).astype(BF)


def _compress_call(kc, vc, w, l):
    B, T, _ = kc.shape
    nc = T // S_CMP
    width = S_CMP * LANE
    kc2 = kc.reshape(B, nc, width)
    vc2 = vc.reshape(B, nc, width)
    consts = [w["pek"], w["pev"], w["w1k"], w["w1v"], w["b1k"], w["b1v"], w["w2k"], w["w2vt"]]
    in_spec = pl.BlockSpec((1, nc, width), lambda b: (b, 0, 0))
    return pl.pallas_call(
        _compress_kernel,
        grid=(B,),
        in_specs=[in_spec, in_spec] + [_layer_spec(c, l) for c in consts],
        out_specs=[pl.BlockSpec((1, NSA_KV_HEADS, nc, LANE), lambda b: (b, 0, 0, 0)),
                   pl.BlockSpec((1, NSA_KV_HEADS, LANE, nc), lambda b: (b, 0, 0, 0))],
        out_shape=[jax.ShapeDtypeStruct((B, NSA_KV_HEADS, nc, LANE), BF),
                   jax.ShapeDtypeStruct((B, NSA_KV_HEADS, LANE, nc), BF)],
        compiler_params=_params("parallel"),
        name="nsa_compress",
    )(kc2, vc2, *consts)


def _cmp_select_kernel(q_ref, kc_ref, vct_ref, ovt_ref, oc_ref, sel_ref, imp_ref, cnt_ref, *, n_sel):
    tq = q_ref.shape[2]
    nc = kc_ref.shape[2]
    ns = ovt_ref.shape[0]
    qs = pl.program_id(2) * tq
    kc = kc_ref[0, 0]
    vct = vct_ref[0, 0]

    t_c = lax.broadcasted_iota(jnp.int32, (nc, tq), 1) + qs
    c_c = lax.broadcasted_iota(jnp.int32, (nc, tq), 0)
    bias = jnp.where(c_c * S_CMP + (L_CMP - 1) <= t_c, 0.0, NEG)
    has_valid = jnp.where(lax.broadcasted_iota(jnp.int32, (1, tq), 1) + qs >= L_CMP - 1, 1.0, 0.0)
    bias = jnp.concatenate([bias] * NSA_HPG, axis=1)
    has_valid = jnp.concatenate([has_valid] * NSA_HPG, axis=1)
    st = _dot_nt(kc, q_ref[0].reshape(NSA_HPG * tq, LANE)) + bias
    et = jnp.exp2(st - st.max(0, keepdims=True))
    p = et * (has_valid / et.sum(0, keepdims=True))
    acc_t = _dot(vct, et.astype(BF))
    o_t = acc_t * (has_valid / acc_t[ONE_LANE:ONE_LANE + 1, :])
    p_sum = p[:, :tq]
    for h in range(1, NSA_HPG):
        p_sum = p_sum + p[:, h * tq:(h + 1) * tq]
    oc_ref[0] = jnp.concatenate([o_t[:, h * tq:(h + 1) * tq].T[:, :NSA_HD] for h in range(NSA_HPG)], axis=1)

    hi = p_sum.astype(BF)
    lo = (p_sum - hi.astype(F32)).astype(BF)
    imp = _dot(ovt_ref[...], hi) + _dot(ovt_ref[...], lo)
    j = lax.broadcasted_iota(jnp.int32, (ns, tq), 0)
    cur = (lax.broadcasted_iota(jnp.int32, (ns, tq), 1) + qs) >> LOG2_L_SLC
    forced = (j == 0) | (j == cur) | (j == cur - 1)
    imp_ref[...] = jnp.where(forced, BIG, jnp.where(j <= cur, imp, NEG))
    cnt_ref[...] = jnp.zeros((ns, tq), F32)
    cur_max = (qs + tq - 1) >> LOG2_L_SLC
    n_grp = ns // SUBLANE
    j_in = lax.broadcasted_iota(jnp.int32, (SUBLANE, tq), 0)
    for kg in range(n_grp):
        @pl.when((kg * SUBLANE <= cur_max) & (cur_max >= n_sel))
        def _():
            grps = [imp_ref[r * SUBLANE:(r + 1) * SUBLANE, :] for r in range(n_grp)]
            cnts = [cnt_ref[r * SUBLANE:(r + 1) * SUBLANE, :] for r in range(n_grp)]
            for k in range(kg * SUBLANE, (kg + 1) * SUBLANE):
                row_k = imp_ref[k:k + 1, :]
                for r, grp in enumerate(grps):
                    if r > kg:
                        ahead = jnp.where(row_k >= grp, 1.0, 0.0)
                    elif r < kg:
                        ahead = jnp.where(row_k > grp, 1.0, 0.0)
                    else:
                        ahead = jnp.where(j_in > k % SUBLANE, jnp.where(row_k >= grp, 1.0, 0.0),
                                          jnp.where(row_k > grp, 1.0, 0.0))
                    cnts[r] = cnts[r] + ahead
            for r in range(n_grp):
                cnt_ref[r * SUBLANE:(r + 1) * SUBLANE, :] = cnts[r]
    not_sel = jnp.where(cnt_ref[...] < n_sel, 0.0, 1.0)
    pieces = [jnp.zeros((ONE_LANE, tq), F32), not_sel]
    if ns < LANE - ONE_LANE:
        pieces.append(jnp.zeros((LANE - ONE_LANE - ns, tq), F32))
    sel_ref[0, 0] = (jnp.concatenate(pieces, axis=0).T * NEG).astype(BF)


def _cmp_select_call(nq, kcc, vct, ovt, n_sel):
    B, H, T, _ = nq.shape
    G = NSA_KV_HEADS
    nc = kcc.shape[2]
    ns = ovt.shape[0]
    tq = min(TQ_CMP, T)
    return pl.pallas_call(
        functools.partial(_cmp_select_kernel, n_sel=n_sel),
        grid=(B, G, T // tq),
        in_specs=[pl.BlockSpec((1, NSA_HPG, tq, LANE), lambda b, g, i: (b, g, i, 0)),
                  pl.BlockSpec((1, 1, nc, LANE), lambda b, g, i: (b, g, 0, 0)),
                  pl.BlockSpec((1, 1, LANE, nc), lambda b, g, i: (b, g, 0, 0)),
                  _const_spec(ovt.shape)],
        out_specs=[pl.BlockSpec((1, tq, NSA_HPG * NSA_HD), lambda b, g, i: (b, i, g)),
                   pl.BlockSpec((1, 1, tq, LANE), lambda b, g, i: (b, g, i, 0))],
        out_shape=[jax.ShapeDtypeStruct((B, T, H * NSA_HD), F32),
                   jax.ShapeDtypeStruct((B, G, T, LANE), BF)],
        scratch_shapes=[pltpu.VMEM((ns, tq), F32), pltpu.VMEM((ns, tq), F32)],
        compiler_params=_params("parallel", "parallel", "parallel"),
        name="nsa_cmp_select",
    )(nq, kcc, vct, ovt)


def _flash_step(q, k, v, m, acc, bias=None):
    s = _dot_nt(q, k)
    if bias is not None:
        reps = s.shape[0] // bias.shape[0]
        s = s + bias if reps == 1 else (s.reshape(reps, *bias.shape) + bias[None]).reshape(s.shape)
    m_new = jnp.maximum(m, s.max(-1, keepdims=True))
    p = jnp.exp2(s - m_new)
    acc = acc * jnp.exp2(m - m_new) + _dot(p.astype(BF), v)
    return m_new, acc


def _diag_biases(tq):
    half = tq // 2
    causal = jnp.where(lax.broadcasted_iota(jnp.int32, (half, half), 1)
                       <= lax.broadcasted_iota(jnp.int32, (half, half), 0), 0.0, NEG)
    return causal, jnp.concatenate([jnp.zeros((half, half), F32), causal], axis=1)


def _causal_tile(q_at, k_at, v_at, c, tq, causal_bias, lower_bias):
    half = tq // 2
    m, acc = jnp.full((tq, 1), NEG, F32), jnp.zeros((tq, LANE), F32)
    for j in range(c):
        m, acc = _flash_step(q_at(slice(None)), k_at(j * tq, tq), v_at(j * tq, tq), m, acc)
    off = c * tq
    _, acc_t = _flash_step(q_at(slice(0, half)), k_at(off, half), v_at(off, half), m[:half], acc[:half], causal_bias)
    _, acc_b = _flash_step(q_at(slice(half, tq)), k_at(off, tq), v_at(off, tq), m[half:], acc[half:], lower_bias)
    return jnp.concatenate([acc_t, acc_b], axis=0)


def _mla_kernel(q_ref, k_ref, v_ref, o_ref):
    nh, tq = q_ref.shape[1], q_ref.shape[2]
    biases = _diag_biases(tq)
    for c in range(k_ref.shape[2] // tq):
        @pl.when(pl.program_id(2) == c)
        def _():
            outs = []
            for hh in range(nh):
                acc = _causal_tile(lambda r: q_ref[0, hh, r, :], lambda s, n: k_ref[0, hh, s:s + n, :],
                                   lambda s, n: v_ref[0, hh, s:s + n, :], c, tq, *biases)
                outs.append(acc[:, :MLA_V] / acc[:, ONE_LANE:ONE_LANE + 1])
            o_ref[0] = jnp.concatenate(outs, axis=1).astype(o_ref.dtype)


def _mla_call(qm, km, vm):
    B, H, T, _ = qm.shape
    tq = min(TQ_MLA, T)
    nh = MLA_HEADS_PER_STEP
    kv_spec = pl.BlockSpec((1, nh, T, LANE), lambda b, h, i: (b, h, 0, 0))
    return pl.pallas_call(
        _mla_kernel,
        grid=(B, H // nh, T // tq),
        in_specs=[pl.BlockSpec((1, nh, tq, LANE), lambda b, h, i: (b, h, i, 0)), kv_spec, kv_spec],
        out_specs=pl.BlockSpec((1, tq, nh * MLA_V), lambda b, h, i: (b, i, h)),
        out_shape=jax.ShapeDtypeStruct((B, T, H * MLA_V), BF),
        compiler_params=_params("parallel", "parallel", "parallel"),
        name="mla_attention",
    )(qm, km, vm)


def _window_bias(q_start, k_start, sub, span):
    qpos = q_start + lax.broadcasted_iota(jnp.int32, (sub, 1), 0)
    kpos = k_start + lax.broadcasted_iota(jnp.int32, (sub, span), 1)
    return jnp.where(kpos <= qpos, jnp.where(kpos > qpos - WINDOW, 0.0, NEG), NEG)


def _nsa_kernel(q_ref, sel_ref, ks_ref, vs_ref, kw_ref, vw_ref, os_ref, ow_ref):
    nh, tq = q_ref.shape[1], q_ref.shape[2]
    biases = _diag_biases(tq)
    sub = min(WIN_SUB, tq)
    span = WINDOW + sub
    far_bias = _window_bias(WINDOW, 0, sub, span)

    def qa(hh, rows):
        return q_ref[0, hh, rows, :] + sel_ref[0, 0, rows, :]

    for c in range(ks_ref.shape[2] // tq):
        @pl.when(pl.program_id(2) == c)
        def _():
            o_s = []
            for hh in range(nh):
                acc = _causal_tile(lambda r: qa(hh, r), lambda s, n: ks_ref[0, 0, s:s + n, :],
                                   lambda s, n: vs_ref[0, 0, s:s + n, :], c, tq, *biases)
                o_s.append(acc[:, :NSA_HD] / acc[:, ONE_LANE:ONE_LANE + 1])
            os_ref[0] = jnp.concatenate(o_s, axis=1)

            for u in range(tq // sub):
                rows = slice(u * sub, (u + 1) * sub)
                q_start = c * tq + u * sub
                ws = max(q_start - WINDOW, 0)
                bias_w = far_bias if q_start >= WINDOW else _window_bias(q_start, ws, sub, span)
                q_u = jnp.concatenate([qa(hh, rows) for hh in range(nh)], axis=0)
                _, acc_w = _flash_step(q_u, kw_ref[0, 0, ws:ws + span, :], vw_ref[0, 0, ws:ws + span, :],
                                       jnp.full((nh * sub, 1), NEG, F32), jnp.zeros((nh * sub, LANE), F32), bias_w)
                o_u = acc_w[:, :NSA_HD] / acc_w[:, ONE_LANE:ONE_LANE + 1]
                ow_ref[0, rows, :] = jnp.concatenate([o_u[hh * sub:(hh + 1) * sub] for hh in range(nh)], axis=1)


def _nsa_call(nq, sel, ks, vs, kw, vw):
    B, H, T, _ = nq.shape
    nh = NSA_HEADS_PER_STEP
    per_grp = NSA_HPG // nh
    tq = min(TQ_NSA, T)
    sub = min(WIN_SUB, tq)
    assert WINDOW % sub == 0 and tq % sub == 0 and T >= WINDOW + sub
    kv_spec = pl.BlockSpec((1, 1, T, LANE), lambda b, h, i: (b, h // per_grp, 0, 0))
    out_spec = pl.BlockSpec((1, tq, nh * NSA_HD), lambda b, h, i: (b, i, h))
    out_shape = jax.ShapeDtypeStruct((B, T, H * NSA_HD), F32)
    return pl.pallas_call(
        _nsa_kernel,
        grid=(B, H // nh, T // tq),
        in_specs=[pl.BlockSpec((1, nh, tq, LANE), lambda b, h, i: (b, h, i, 0)),
                  pl.BlockSpec((1, 1, tq, LANE), lambda b, h, i: (b, h // per_grp, i, 0)),
                  kv_spec, kv_spec, kv_spec, kv_spec],
        out_specs=[out_spec, out_spec],
        out_shape=[out_shape, out_shape],
        compiler_params=_params("parallel", "parallel", "parallel"),
        name="nsa_select_window",
    )(nq, sel, ks, vs, kw, vw)


def _merge_kernel(x_ref, om_ref, oc_ref, os_ref, ow_ref, gb_ref, wgm_ref, wgn_ref, wpm_ref, wpn_ref, wo_ref,
                  g_ref, b_ref, wg2_ref, wu2_ref, wd2_ref, g2_ref, b2_ref, o_ref):
    x = x_ref[...]
    xb = x.astype(BF)
    gates = gb_ref[...]
    tm = gates.shape[0]
    head = lax.broadcasted_iota(jnp.int32, (tm, LANE), 1) // NSA_HD
    slabs = []
    for j in range(NSA_HEADS * NSA_HD // LANE):
        sl = slice(j * LANE, (j + 1) * LANE)
        idx = 3 * (head + j * (LANE // NSA_HD))
        slabs.append(jnp.take_along_axis(gates, idx, axis=1) * oc_ref[:, sl]
                     + jnp.take_along_axis(gates, idx + 1, axis=1) * os_ref[:, sl]
                     + jnp.take_along_axis(gates, idx + 2, axis=1) * ow_ref[:, sl])
    o_n = jnp.concatenate(slabs, axis=1).astype(BF)
    y = (jax.nn.sigmoid(_dot(xb, wgm_ref[...])) * _dot(om_ref[...], wpm_ref[...])
         + jax.nn.sigmoid(_dot(xb, wgn_ref[...])) * _dot(o_n, wpn_ref[...]))
    mix = _dot(y.astype(BF), wo_ref[...])
    x2 = _ln(ALPHA * x + mix, g_ref[...], b_ref[...])
    o_ref[...] = _ffn_ln(x2, wg2_ref, wu2_ref, wd2_ref, g2_ref[...], b2_ref[...])


def _merge_call(x2d, om, oc, o_s, o_w, gates, w, l):
    n, d = x2d.shape
    tm = min(TM_ROWS, n)
    row = lambda a: pl.BlockSpec((tm, a.shape[1]), lambda i: (i, 0))
    consts = [w["wgm"], w["wgn"], w["wpm"], w["wpn"], w["wo"], w["ln_mix_g"], w["ln_mix_b"],
              w["ffn2wg"], w["ffn2wu"], w["ffn2wd"], w["ln_f2_g"], w["ln_f2_b"]]
    return pl.pallas_call(
        _merge_kernel,
        grid=(n // tm,),
        in_specs=[row(a) for a in (x2d, om, oc, o_s, o_w, gates)] + [_layer_spec(c, l) for c in consts],
        out_specs=pl.BlockSpec((tm, d), lambda i: (i, 0)),
        out_shape=jax.ShapeDtypeStruct((n, d), F32),
        compiler_params=_params("parallel"),
        name="merge_out_ln",
    )(x2d, om, oc, o_s, o_w, gates, *consts)


def _pad_last(w, width):
    return jnp.pad(w, [(0, 0)] * (w.ndim - 1) + [(0, width - w.shape[-1])])


def _stacked_weights(p):
    w_in = p["w_in"]
    nl, d = w_in.shape[:2]
    w = {}
    w["wa"] = jnp.concatenate([w_in[..., OFF_CQ:OFF_KPE], jnp.zeros((nl, d, MLA_NOPE), F32),
                               w_in[..., OFF_KPE:OFF_NQ],
                               jnp.zeros((nl, d, LANE - MLA_NOPE - MLA_ROPE), F32)], axis=-1).astype(BF)
    w["wn"] = jnp.concatenate([w_in[..., OFF_NQ:OFF_GN], _pad_last(w_in[..., OFF_GN:OFF_GM], LANE)], axis=-1).astype(BF)

    w_uq = p["w_uq"].reshape(nl, MLA_Q_RANK, MLA_HEADS, MLA_NOPE + MLA_ROPE)
    w["wq"] = _pad_last(w_uq, LANE).reshape(nl, MLA_Q_RANK, -1).astype(BF)
    pe1, pe2 = w_uq[..., MLA_NOPE:MLA_NOPE + MLA_ROPE // 2], w_uq[..., MLA_NOPE + MLA_ROPE // 2:]
    w["wqr"] = jnp.pad(jnp.concatenate([-pe2, pe1], axis=-1),
                       ((0, 0), (0, 0), (0, 0), (MLA_NOPE, LANE - MLA_NOPE - MLA_ROPE))
                       ).reshape(nl, MLA_Q_RANK, -1).astype(BF)
    w_ukv = p["w_ukv"].reshape(nl, MLA_KV_RANK, MLA_HEADS, MLA_NOPE + MLA_V)
    w["wk"] = _pad_last(w_ukv[..., :MLA_NOPE], LANE).reshape(nl, MLA_KV_RANK, -1).astype(BF)
    w["wv"] = _pad_last(w_ukv[..., MLA_NOPE:], LANE).reshape(nl, MLA_KV_RANK, -1).astype(BF)
    w["gq"] = p["q_norm_g"][:, None]
    w["gkv"] = p["kv_norm_g"][:, None]

    G = NSA_KV_HEADS
    own_group = jnp.eye(G, dtype=F32).reshape(1, G, 1, 1, G, 1, 1)
    for nm, pe, w1, b1, w2 in (("k", "cmp_pe_k", "cmp_k_w1", "cmp_k_b1", "cmp_k_w2"),
                               ("v", "cmp_pe_v", "cmp_v_w1", "cmp_v_b1", "cmp_v_w2")):
        pe_l = p[pe].reshape(nl, 2, S_CMP, 1, NSA_HD)
        w["pe" + nm] = jnp.broadcast_to(pe_l, (nl, 2, S_CMP, G, NSA_HD)).reshape(nl, 2, S_CMP * LANE)
        w1_l = p[w1].reshape(nl, 1, 2, S_CMP, 1, NSA_HD, CMP_HIDDEN)
        w["w1" + nm] = (w1_l * own_group).reshape(nl, G, 2, S_CMP * LANE, CMP_HIDDEN).astype(BF)
        w["b1" + nm] = p[b1][:, None]
        w2_l = _pad_last(p[w2], LANE).astype(BF)
        w["w2" + nm] = w2_l
        w["w2" + nm + "t"] = jnp.swapaxes(w2_l, 1, 2)

    w["wgm"] = w_in[..., OFF_GM:OFF_GNS].astype(BF)
    w["wgn"] = w_in[..., OFF_GNS:].astype(BF)
    w["wpm"] = p["w_proj_mla"].astype(BF)
    w["wpn"] = p["w_proj_nsa"].astype(BF)
    w["wo"] = p["w_out"].astype(BF)
    w["ln_mix_g"] = p["ln_mix_g"][:, None]
    w["ln_mix_b"] = p["ln_mix_b"][:, None]
    for nm in ("ffn1", "ffn2"):
        for s in ("wg", "wu", "wd"):
            w[nm + s] = p[nm + "_" + s].astype(BF)
    for nm in ("ln_f1", "ln_f2"):
        w[nm + "_g"] = p[nm + "_g"][:, None]
        w[nm + "_b"] = p[nm + "_b"][:, None]
    return w


def _rope_tables(T):
    def tables(rot_dim):
        inv = 1.0 / (ROPE_THETA ** (jnp.arange(0, rot_dim, 2, dtype=F32) / rot_dim))
        ang = jnp.arange(T, dtype=F32)[:, None] * inv[None, :]
        return jnp.cos(ang), jnp.sin(ang)

    cos_m, sin_m = tables(MLA_ROPE)
    cos_n, sin_n = tables(NSA_ROT)
    one = lambda n: jnp.ones((T, n), F32)
    zero = lambda n: jnp.zeros((T, n), F32)
    hm, hn = MLA_ROPE // 2, NSA_ROT // 2
    rest_m = LANE - MLA_NOPE - MLA_ROPE
    cm = jnp.concatenate([one(MLA_NOPE), cos_m, cos_m, one(rest_m)], 1)
    s1m = jnp.concatenate([zero(MLA_NOPE), -sin_m, zero(hm), zero(rest_m)], 1)
    s2m = jnp.concatenate([zero(MLA_NOPE), zero(hm), sin_m, zero(rest_m)], 1)
    rest_n = NSA_HD - NSA_ROT
    cn = jnp.tile(jnp.concatenate([cos_n, cos_n, one(rest_n)], 1), (1, 2))
    s1n = jnp.tile(jnp.concatenate([-sin_n, zero(hn), zero(rest_n)], 1), (1, 2))
    s2n = jnp.tile(jnp.concatenate([zero(hn), sin_n, zero(rest_n)], 1), (1, 2))
    return cm, s1m, s2m, cn, s1n, s2n


def _overlap_t(T):
    nc, ns = T // S_CMP, T // L_SLC
    n_cmp = (T - L_CMP) // S_CMP + 1
    start = np.arange(nc) * S_CMP
    blk = np.arange(ns)
    ov = ((start[None, :] < (blk[:, None] + 1) * L_SLC) & (start[None, :] + L_CMP > blk[:, None] * L_SLC)
          & (np.arange(nc)[None, :] < n_cmp))
    return jnp.asarray(ov.astype(np.float32), dtype=BF)


def kernel(x, ln_f1_g, ln_f1_b, ffn1_wg, ffn1_wu, ffn1_wd, w_in, q_norm_g, w_uq, kv_norm_g, w_ukv,
           cmp_pe_k, cmp_k_w1, cmp_k_b1, cmp_k_w2, cmp_pe_v, cmp_v_w1, cmp_v_b1, cmp_v_w2,
           w_proj_mla, w_proj_nsa, w_out, ln_mix_g, ln_mix_b,
           ffn2_wg, ffn2_wu, ffn2_wd, ln_f2_g, ln_f2_b):
    p = dict(ln_f1_g=ln_f1_g, ln_f1_b=ln_f1_b, ffn1_wg=ffn1_wg, ffn1_wu=ffn1_wu, ffn1_wd=ffn1_wd,
             w_in=w_in, q_norm_g=q_norm_g, w_uq=w_uq, kv_norm_g=kv_norm_g, w_ukv=w_ukv,
             cmp_pe_k=cmp_pe_k, cmp_k_w1=cmp_k_w1, cmp_k_b1=cmp_k_b1, cmp_k_w2=cmp_k_w2,
             cmp_pe_v=cmp_pe_v, cmp_v_w1=cmp_v_w1, cmp_v_b1=cmp_v_b1, cmp_v_w2=cmp_v_w2,
             w_proj_mla=w_proj_mla, w_proj_nsa=w_proj_nsa, w_out=w_out,
             ln_mix_g=ln_mix_g, ln_mix_b=ln_mix_b, ffn2_wg=ffn2_wg, ffn2_wu=ffn2_wu, ffn2_wd=ffn2_wd,
             ln_f2_g=ln_f2_g, ln_f2_b=ln_f2_b)
    B, T, D = x.shape
    assert T % L_SLC == 0 and T // L_SLC <= LANE - ONE_LANE
    tabs = _rope_tables(T)
    ovt = _overlap_t(T)
    n_sel = min(N_SEL, T // L_SLC)
    h = x.reshape(B * T, D)
    w = _stacked_weights(p)
    for l in range(DEPTH):
        h = _ffn_call(h, l, w["ffn1wg"], w["ffn1wu"], w["ffn1wd"], w["ln_f1_g"], w["ln_f1_b"])
        qm, km, vm, nq, kc, vc, ks, vs, kw, vw, gn = _inproj_call(h.reshape(B, T, D), w, l, tabs)
        kcc, vct = _compress_call(kc, vc, w, l)
        oc, sel = _cmp_select_call(nq, kcc, vct, ovt, n_sel)
        om = _mla_call(qm, km, vm)
        o_s, o_w = _nsa_call(nq, sel, ks, vs, kw, vw)
        flat = lambda a: a.reshape(B * T, -1)
        h = _merge_call(h, flat(om), flat(oc), flat(o_s), flat(o_w), flat(gn), w, l)
    return h.reshape(B, T, D)
```
